```python
import math
import jax, jax.numpy as jnp
from jax import lax
import numpy as np

D_MODEL = 1024
BATCH = 32
SEQ = 2048
DEPTH = 2

HEAD_DIM = 64
A_HEADS = 6
A_KV_HEADS = 2
B_HEADS = 6
B_KV_HEADS = 2
C_HEADS = 4
C_Q_RANK = 256
C_KV_RANK = 128
C_NOPE_DIM = 64
C_ROPE_DIM = 32
C_V_DIM = 64
C_QK_DIM = C_NOPE_DIM + C_ROPE_DIM
D_FF = 2816
GRID_W = 64
Q_BLOCK = 128
WINDOW = 128
NUM_BUCKETS = 32
MAX_DISTANCE = 128
ROPE_THETA = 10000.0
ADA_CHUNKS = 9
EPS = 1e-6
NEG_INF = -1e30

A_Q_W = A_HEADS * HEAD_DIM
A_KV_W = A_KV_HEADS * HEAD_DIM
B_Q_W = B_HEADS * HEAD_DIM
B_KV_W = B_KV_HEADS * HEAD_DIM
IN_SIZES = (A_Q_W, A_KV_W, A_KV_W, B_Q_W, B_KV_W, B_KV_W,
            C_Q_RANK, C_KV_RANK, C_ROPE_DIM, D_MODEL, D_MODEL, D_MODEL)
IN_COLS = (A_Q_W + 2 * A_KV_W + B_Q_W + 2 * B_KV_W
           + C_Q_RANK + C_KV_RANK + C_ROPE_DIM + 3 * D_MODEL)

kernel_name = 'hybrid_gated_mixer_encoder'


def rms_norm(x, g):
    xf = x.astype(jnp.float32)
    y = xf * lax.rsqrt(jnp.mean(xf * xf, axis=-1, keepdims=True) + EPS)
    return (y * g.astype(jnp.float32)).astype(x.dtype)


def modulate(h, shift, scale):
    return h * (1.0 + scale[:, None, :]) + shift[:, None, :]


def swiglu(h, w_gu, w_down):
    gate, up = jnp.split(h @ w_gu, 2, axis=-1)
    return (jax.nn.silu(gate) * up) @ w_down


def split_cols(y, sizes):
    out, off = [], 0
    for s in sizes:
        out.append(y[..., off:off + s])
        off += s
    return out


def rope_angles(pos, dim):
    inv = ROPE_THETA ** (-jnp.arange(0, dim, 2, dtype=jnp.float32) / dim)
    ang = pos.astype(jnp.float32)[:, None] * inv[None, :]
    return jnp.cos(ang), jnp.sin(ang)


def apply_rope(x, cos, sin):
    half = x.shape[-1] // 2
    xf = x.astype(jnp.float32)
    x1, x2 = xf[..., :half], xf[..., half:]
    cos, sin = cos[:, None, :], sin[:, None, :]
    return jnp.concatenate([x1 * cos - x2 * sin, x1 * sin + x2 * cos], axis=-1).astype(x.dtype)


def axial_rope(x, row_cs, col_cs):
    half = x.shape[-1] // 2
    return jnp.concatenate([apply_rope(x[..., :half], *row_cs),
                            apply_rope(x[..., half:], *col_cs)], axis=-1)


def t5_bucket(rel):
    nb = NUM_BUCKETS // 2
    max_exact = nb // 2
    ret = jnp.where(rel > 0, nb, 0)
    n = jnp.abs(rel)
    large = max_exact + (jnp.log(jnp.maximum(n, 1).astype(jnp.float32) / max_exact)
                         / math.log(MAX_DISTANCE / max_exact) * (nb - max_exact)).astype(jnp.int32)
    large = jnp.minimum(large, nb - 1)
    return ret + jnp.where(n < max_exact, n, large)


def window_bias_mask(rel_bias, seq):
    nblk = seq // Q_BLOCK
    r = jnp.arange(Q_BLOCK)[:, None]
    j = jnp.arange(3 * Q_BLOCK)[None, :]
    rel = j - Q_BLOCK - r
    bias = rel_bias[t5_bucket(rel)].astype(jnp.float32)
    bias = jnp.transpose(bias, (2, 0, 1)).reshape(B_KV_HEADS, B_HEADS // B_KV_HEADS, Q_BLOCK, 3 * Q_BLOCK)
    kpos = jnp.arange(nblk)[:, None, None] * Q_BLOCK - Q_BLOCK + j[None]
    mask = (jnp.abs(rel) <= WINDOW)[None] & (kpos >= 0) & (kpos < seq)
    return bias, mask


def dense_attention(q, k, v, scale):
    b, s, kh, g, dq = q.shape
    nblk = s // Q_BLOCK
    qb = jnp.moveaxis(q.reshape(b, nblk, Q_BLOCK, kh, g, dq), 1, 0)

    def one_block(qi):
        logits = jnp.einsum('bqkgd,bskd->bkgqs', qi, k).astype(jnp.float32) * scale
        p = jax.nn.softmax(logits, axis=-1).astype(v.dtype)
        return jnp.einsum('bkgqs,bskd->bqkgd', p, v)

    out = jnp.moveaxis(lax.map(one_block, qb), 0, 1)
    return out.reshape(b, s, kh * g * v.shape[-1])


def window_attention(q, k, v, bias, mask, sink, scale):
    b, s, kh, g, d = q.shape
    nblk = s // Q_BLOCK
    qb = q.reshape(b, nblk, Q_BLOCK, kh, g, d)

    def band(a):
        pad = jnp.pad(a, ((0, 0), (Q_BLOCK, Q_BLOCK), (0, 0), (0, 0)))
        pb = pad.reshape(b, nblk + 2, Q_BLOCK, kh, a.shape[-1])
        return jnp.concatenate([pb[:, :-2], pb[:, 1:-1], pb[:, 2:]], axis=2)

    kb, vb = band(k), band(v)
    logits = jnp.einsum('bnqkgd,bnskd->bnkgqs', qb, kb).astype(jnp.float32) * scale + bias[None, None]
    logits = jnp.where(mask[None, :, None, None], logits, NEG_INF)
    sink_col = jnp.broadcast_to(sink.astype(jnp.float32).reshape(1, 1, kh, g, 1, 1),
                                logits.shape[:-1] + (1,))
    p = jax.nn.softmax(jnp.concatenate([logits, sink_col], axis=-1), axis=-1)[..., :-1]
    out = jnp.einsum('bnkgqs,bnskd->bnqkgd', p.astype(v.dtype), vb)
    return out.reshape(b, s, kh * g * d)


def token_mix(h, w_in, a_q_norm, a_k_norm, b_sink, c_q_lat_norm, c_w_q_up, c_kv_lat_norm,
              c_w_kv_up, w_br_a, w_br_b, w_br_c, w_out, row_cs, col_cs, seq_cs, win_bias, win_mask):
    b, s, _ = h.shape
    (aq, ak, av, bq, bk, bv, cq_lat, ckv_lat, ck_rope,
     gate_a, gate_b, gate_c) = split_cols(h @ w_in, IN_SIZES)

    qa = axial_rope(rms_norm(aq.reshape(b, s, A_HEADS, HEAD_DIM), a_q_norm), row_cs, col_cs)
    ka = axial_rope(rms_norm(ak.reshape(b, s, A_KV_HEADS, HEAD_DIM), a_k_norm), row_cs, col_cs)
    va = av.reshape(b, s, A_KV_HEADS, HEAD_DIM)
    qa = qa.reshape(b, s, A_KV_HEADS, A_HEADS // A_KV_HEADS, HEAD_DIM)
    o_a = dense_attention(qa, ka, va, HEAD_DIM ** -0.5)

    qb = bq.reshape(b, s, B_KV_HEADS, B_HEADS // B_KV_HEADS, HEAD_DIM)
    kb = bk.reshape(b, s, B_KV_HEADS, HEAD_DIM)
    vb = bv.reshape(b, s, B_KV_HEADS, HEAD_DIM)
    o_b = window_attention(qb, kb, vb, win_bias, win_mask, b_sink, HEAD_DIM ** -0.5)

    qc = (rms_norm(cq_lat, c_q_lat_norm) @ c_w_q_up).reshape(b, s, C_HEADS, C_QK_DIM)
    qc_nope, qc_rope = qc[..., :C_NOPE_DIM], apply_rope(qc[..., C_NOPE_DIM:], *seq_cs)
    kv = (rms_norm(ckv_lat, c_kv_lat_norm) @ c_w_kv_up).reshape(b, s, C_HEADS, C_NOPE_DIM + C_V_DIM)
    kc_nope, vc = kv[..., :C_NOPE_DIM], kv[..., C_NOPE_DIM:]
    kc_rope = jnp.broadcast_to(apply_rope(ck_rope[:, :, None, :], *seq_cs), (b, s, C_HEADS, C_ROPE_DIM))
    qc_full = jnp.concatenate([qc_nope, qc_rope], axis=-1).reshape(b, s, C_HEADS, 1, C_QK_DIM)
    kc_full = jnp.concatenate([kc_nope, kc_rope], axis=-1)
    o_c = dense_attention(qc_full, kc_full, vc, C_QK_DIM ** -0.5)

    merged = (jax.nn.sigmoid(gate_a) * (o_a @ w_br_a)
              + jax.nn.sigmoid(gate_b) * (o_b @ w_br_b)
              + jax.nn.sigmoid(gate_c) * (o_c @ w_br_c))
    return merged @ w_out


def setup_inputs(seed: int = 0) -> dict:
    key = jax.random.key(seed)
    ks = jax.random.split(key, 32)
    L, D = DEPTH, D_MODEL

    def nrm(k, shape, scale):
        return jax.random.normal(k, shape, jnp.float32) * scale

    def gain(k, shape):
        return 1.0 + 0.05 * jax.random.normal(k, shape, jnp.float32)

    return {
        'x': nrm(ks[0], (BATCH, SEQ, D), 1.0),
        'c': nrm(ks[1], (BATCH, D), 1.0),
        'ada_w': nrm(ks[2], (L, D, ADA_CHUNKS * D), 0.5 * D ** -0.5),
        'ada_b': nrm(ks[3], (L, ADA_CHUNKS * D), 0.02),
        'norm_ffn1': gain(ks[4], (L, D)),
        'ffn1_w_gu': nrm(ks[5], (L, D, 2 * D_FF), D ** -0.5),
        'ffn1_w_down': nrm(ks[6], (L, D_FF, D), D_FF ** -0.5),
        'norm_mix': gain(ks[7], (L, D)),
        'w_in': nrm(ks[8], (L, D, IN_COLS), D ** -0.5),
        'a_q_norm': gain(ks[9], (L, HEAD_DIM)),
        'a_k_norm': gain(ks[10], (L, HEAD_DIM)),
        'b_sink': nrm(ks[11], (L, B_HEADS), 1.0),
        'rel_bias': nrm(ks[12], (NUM_BUCKETS, B_HEADS), 0.5),
        'c_q_lat_norm': gain(ks[13], (L, C_Q_RANK)),
        'c_w_q_up': nrm(ks[14], (L, C_Q_RANK, C_HEADS * C_QK_DIM), C_Q_RANK ** -0.5),
        'c_kv_lat_norm': gain(ks[15], (L, C_KV_RANK)),
        'c_w_kv_up': nrm(ks[16], (L, C_KV_RANK, C_HEADS * (C_NOPE_DIM + C_V_DIM)), C_KV_RANK ** -0.5),
        'w_br_a': nrm(ks[17], (L, A_Q_W, D), A_Q_W ** -0.5),
        'w_br_b': nrm(ks[18], (L, B_Q_W, D), B_Q_W ** -0.5),
        'w_br_c': nrm(ks[19], (L, C_HEADS * C_V_DIM, D), (C_HEADS * C_V_DIM) ** -0.5),
        'w_out': nrm(ks[20], (L, D, D), D ** -0.5),
        'norm_ffn2': gain(ks[21], (L, D)),
        'ffn2_w_gu': nrm(ks[22], (L, D, 2 * D_FF), D ** -0.5),
        'ffn2_w_down': nrm(ks[23], (L, D_FF, D), D_FF ** -0.5),
        'final_norm': gain(ks[24], (D,)),
    }


def reference(x, c, ada_w, ada_b, norm_ffn1, ffn1_w_gu, ffn1_w_down, norm_mix, w_in,
              a_q_norm, a_k_norm, b_sink, rel_bias, c_q_lat_norm, c_w_q_up, c_kv_lat_norm,
              c_w_kv_up, w_br_a, w_br_b, w_br_c, w_out, norm_ffn2, ffn2_w_gu, ffn2_w_down,
              final_norm):
    b, s, _ = x.shape
    rows = s // GRID_W
    t = jnp.arange(s)
    row_pos = jnp.repeat(jnp.arange(rows), GRID_W)
    col_pos = jnp.tile(jnp.arange(GRID_W), rows)
    row_cs = rope_angles(row_pos, HEAD_DIM // 2)
    col_cs = rope_angles(col_pos, HEAD_DIM // 2)
    seq_cs = rope_angles(t, C_ROPE_DIM)
    win_bias, win_mask = window_bias_mask(rel_bias, s)
    cond = jax.nn.silu(c)

    for l in range(DEPTH):
        mods = cond @ ada_w[l] + ada_b[l]
        sh1, sc1, g1, sh2, sc2, g2, sh3, sc3, g3 = jnp.split(mods, ADA_CHUNKS, axis=-1)

        h = modulate(rms_norm(x, norm_ffn1[l]), sh1, sc1)
        x = x + 0.5 * g1[:, None, :] * swiglu(h, ffn1_w_gu[l], ffn1_w_down[l])

        h = modulate(rms_norm(x, norm_mix[l]), sh2, sc2)
        x = x + g2[:, None, :] * token_mix(
            h, w_in[l], a_q_norm[l], a_k_norm[l], b_sink[l], c_q_lat_norm[l], c_w_q_up[l],
            c_kv_lat_norm[l], c_w_kv_up[l], w_br_a[l], w_br_b[l], w_br_c[l], w_out[l],
            row_cs, col_cs, seq_cs, win_bias, win_mask)

        h = modulate(rms_norm(x, norm_ffn2[l]), sh3, sc3)
        x = x + 0.5 * g3[:, None, :] * swiglu(h, ffn2_w_gu[l], ffn2_w_down[l])

    return rms_norm(x, final_norm)
```

```python
import functools
import math

import numpy as np
import jax
import jax.numpy as jnp
from jax import lax
from jax.experimental import pallas as pl
from jax.experimental.pallas import tpu as pltpu

D_MODEL = 1024
DEPTH = 2
HEAD_DIM = 64
A_HEADS = 6
A_KV_HEADS = 2
B_HEADS = 6
B_KV_HEADS = 2
C_HEADS = 4
C_Q_RANK = 256
C_KV_RANK = 128
C_NOPE_DIM = 64
C_ROPE_DIM = 32
C_V_DIM = 64
C_QK_DIM = C_NOPE_DIM + C_ROPE_DIM
D_FF = 2816
GRID_W = 64
WINDOW = 128
NUM_BUCKETS = 32
MAX_DISTANCE = 128
ROPE_THETA = 10000.0
ADA_CHUNKS = 9
EPS = 1e-6
NEG_INF = -1e30

A_Q_W = A_HEADS * HEAD_DIM
A_KV_W = A_KV_HEADS * HEAD_DIM
B_Q_W = B_HEADS * HEAD_DIM
B_KV_W = B_KV_HEADS * HEAD_DIM
IN_SIZES = (A_Q_W, A_KV_W, A_KV_W, B_Q_W, B_KV_W, B_KV_W,
            C_Q_RANK, C_KV_RANK, C_ROPE_DIM, D_MODEL, D_MODEL, D_MODEL)
IN_OFFS = tuple(int(v) for v in np.cumsum((0,) + IN_SIZES))
GATE_OFF = IN_OFFS[9]

LANES = 128
V7X_VMEM_LIMIT_BYTES = 60000 * 1024

ROW_TILE = 256
Q_TILE = 256
WIN_KEYS = Q_TILE + 2 * WINDOW
TAB_W = Q_TILE + 4 * WINDOW

BF16 = jnp.bfloat16
F32 = jnp.float32


def _params(*sem):
    return pltpu.CompilerParams(dimension_semantics=sem, vmem_limit_bytes=V7X_VMEM_LIMIT_BYTES)


def _const_spec(shape):
    nd = len(shape)
    return pl.BlockSpec(shape, lambda *_: (0,) * nd, pipeline_mode=pl.Buffered(1))


def _sigmoid(v):
    return 1.0 / (1.0 + jnp.exp(-v))


def _norm_mod(x, gain, shift, scale):
    ms = jnp.mean(x * x, axis=-1, keepdims=True)
    y = x * lax.rsqrt(ms + EPS) * gain
    return y * (1.0 + scale) + shift


def _ada_kernel(c_ref, w_ref, b_ref, o_ref):
    c = c_ref[...]
    cond = (c * _sigmoid(c)).astype(BF16)
    o_ref[0] = jnp.dot(cond, w_ref[0].astype(BF16), preferred_element_type=F32) + b_ref[0]


def _ada_mods(c, ada_w, ada_b):
    depth, d, _ = ada_w.shape
    b = c.shape[0]
    bias = ada_b.reshape(depth * ADA_CHUNKS, 1, d)
    out = pl.pallas_call(
        _ada_kernel,
        grid=(depth, ADA_CHUNKS),
        in_specs=[
            pl.BlockSpec((b, d), lambda l, j: (0, 0)),
            pl.BlockSpec((1, d, d), lambda l, j: (l, 0, j)),
            pl.BlockSpec((1, 1, d), lambda l, j: (l * ADA_CHUNKS + j, 0, 0)),
        ],
        out_specs=pl.BlockSpec((1, b, d), lambda l, j: (l * ADA_CHUNKS + j, 0, 0)),
        out_shape=jax.ShapeDtypeStruct((depth * ADA_CHUNKS, b, d), F32),
        compiler_params=_params("arbitrary", "arbitrary"),
        name="ada_mods",
    )(c, ada_w, bias)
    return jnp.transpose(out, (1, 0, 2))[:, :, None, :]


def _ffn_kernel(x_ref, m_ref, g_ref, wgu_ref, wd_ref, *rest, final):
    o_ref = rest[-1]
    x = x_ref[0]
    h = _norm_mod(x, g_ref[...], m_ref[0, 0], m_ref[0, 1]).astype(BF16)
    gu = jnp.dot(h, wgu_ref[...], preferred_element_type=F32)
    gate = gu[:, :D_FF]
    up = gu[:, D_FF:]
    act = (gate * _sigmoid(gate) * up).astype(BF16)
    y = jnp.dot(act, wd_ref[...], preferred_element_type=F32)
    out = x + (0.5 * m_ref[0, 2]) * y
    if final:
        ms = jnp.mean(out * out, axis=-1, keepdims=True)
        out = out * lax.rsqrt(ms + EPS) * rest[0][...]
    o_ref[0] = out


def _ffn(x, mods, mod_idx, gain, w_gu, w_down, final_gain=None):
    b, s, d = x.shape
    final = final_gain is not None
    in_specs = [
        pl.BlockSpec((1, ROW_TILE, d), lambda i, j: (i, j, 0)),
        pl.BlockSpec((1, 3, 1, d), lambda i, j: (i, mod_idx, 0, 0)),
        _const_spec((1, d)),
        _const_spec((d, 2 * D_FF)),
        _const_spec((D_FF, d)),
    ]
    args = [x, mods, gain.reshape(1, d), w_gu.astype(BF16), w_down.astype(BF16)]
    if final:
        in_specs.append(_const_spec((1, d)))
        args.append(final_gain.reshape(1, d))
    return pl.pallas_call(
        functools.partial(_ffn_kernel, final=final),
        grid=(b, s // ROW_TILE),
        in_specs=in_specs,
        out_specs=pl.BlockSpec((1, ROW_TILE, d), lambda i, j: (i, j, 0)),
        out_shape=jax.ShapeDtypeStruct((b, s, d), F32),
        compiler_params=_params("arbitrary", "arbitrary"),
        name="ffn_final" if final else "ffn",
    )(*args)


_SEG_SIZES = (A_Q_W, A_Q_W, A_KV_W, A_KV_W, A_KV_W, B_Q_W, B_KV_W, B_KV_W,
              C_Q_RANK, C_KV_RANK, LANES, LANES)
_SEG_OFFS = tuple(int(v) for v in np.cumsum((0,) + _SEG_SIZES))
PROJ_W = _SEG_OFFS[-1]
Q_SLOTS = A_HEADS // A_KV_HEADS


def _q_head_order():
    g = A_HEADS // A_KV_HEADS
    return [k * g + j for j in range(g) for k in range(A_KV_HEADS)]


def _proj_columns():
    aq, ak, av, bq, bk, bv, cq, ckv, ckr = IN_OFFS[:9]
    order = _q_head_order()
    dims = np.arange(HEAD_DIM)
    partner = dims ^ (HEAD_DIM // 4)
    cols = []
    cols += [aq + h * HEAD_DIM + d for h in order for d in dims]
    cols += [aq + h * HEAD_DIM + d for h in order for d in partner]
    cols += [ak + h * HEAD_DIM + d for h in range(A_KV_HEADS) for d in dims]
    cols += [ak + h * HEAD_DIM + d for h in range(A_KV_HEADS) for d in partner]
    cols += list(range(av, av + A_KV_W))
    cols += [bq + h * HEAD_DIM + d for h in order for d in dims]
    cols += list(range(bk, bk + B_KV_W))
    cols += list(range(bv, bv + B_KV_W))
    cols += list(range(cq, cq + C_Q_RANK))
    cols += list(range(ckv, ckv + C_KV_RANK))
    rdims = np.arange(C_ROPE_DIM)
    rpartner = rdims ^ (C_ROPE_DIM // 2)
    pad_hi = LANES - C_NOPE_DIM - C_ROPE_DIM
    cols += [-1] * C_NOPE_DIM + [ckr + d for d in rdims] + [-1] * pad_hi
    cols += [-1] * C_NOPE_DIM + [ckr + d for d in rpartner] + [-1] * pad_hi
    cols = np.asarray(cols, np.int32)
    assert cols.shape[0] == PROJ_W
    return cols


def _take_cols(w, cols):
    picked = jnp.take(w, jnp.asarray(np.maximum(cols, 0)), axis=1)
    return jnp.where(jnp.asarray(cols >= 0)[None, :], picked, 0.0)


def _q_up_columns():
    rdims = np.arange(C_ROPE_DIM)
    rpartner = rdims ^ (C_ROPE_DIM // 2)
    pad_hi = LANES - C_QK_DIM
    direct, part = [], []
    for h in range(C_HEADS):
        base = h * C_QK_DIM
        direct += [base + d for d in range(C_NOPE_DIM)] + [base + C_NOPE_DIM + d for d in rdims] + [-1] * pad_hi
        part += [-1] * C_NOPE_DIM + [base + C_NOPE_DIM + d for d in rpartner] + [-1] * pad_hi
    return np.asarray(direct + part, np.int32)


def _kv_up_columns():
    per = C_NOPE_DIM + C_V_DIM
    keys, vals = [], []
    for h in range(C_HEADS):
        keys += [h * per + d for d in range(C_NOPE_DIM)] + [-1] * (LANES - C_NOPE_DIM)
        vals += [h * per + C_NOPE_DIM + d for d in range(C_V_DIM)]
    return np.asarray(keys + vals, np.int32)


C_Q_W = C_HEADS * LANES
C_V_W = C_HEADS * C_V_DIM


def _half_rms(t, n):
    lane = lax.broadcasted_iota(jnp.int32, t.shape, 1)
    left = lane < HEAD_DIM
    sq = t * t
    s_l = jnp.sum(jnp.where(left, sq, 0.0), axis=-1, keepdims=True)
    s_r = jnp.sum(jnp.where(left, 0.0, sq), axis=-1, keepdims=True)
    return jnp.where(left, lax.rsqrt(s_l / n + EPS), lax.rsqrt(s_r / n + EPS))


def _proj_kernel(x_ref, m_ref, g_ref, w_ref, aqn_ref, akn_ref, cqn_ref, wq_ref, ckvn_ref, wkv_ref,
                 ac_ref, as_ref, t1_ref, t2_ref,
                 qa_ref, ka_ref, va_ref, qb_ref, kb_ref, vb_ref, qc_ref, kc_ref, vc_ref):
    x = x_ref[0]
    h = _norm_mod(x, g_ref[...], m_ref[0, 0], m_ref[0, 1]).astype(BF16)
    p = jnp.dot(h, w_ref[...], preferred_element_type=F32)
    seg = [p[:, _SEG_OFFS[i]:_SEG_OFFS[i + 1]] for i in range(len(_SEG_SIZES))]
    aq, aqp, ak, akp, av, bq, bk, bv, cq, ckv, kr1, kr2 = seg

    cos_a = ac_ref[...]
    sin_a = as_ref[...]
    q_scale = HEAD_DIM ** -0.5
    gq_c = aqn_ref[0:1, :] * cos_a
    gq_s = aqn_ref[1:2, :] * sin_a
    for j in range(Q_SLOTS):
        sl = slice(j * LANES, (j + 1) * LANES)
        r = _half_rms(aq[:, sl], float(HEAD_DIM))
        qa_ref[0, :, sl] = ((aq[:, sl] * gq_c + aqp[:, sl] * gq_s) * (r * q_scale)).astype(BF16)
        qb_ref[0, :, sl] = (bq[:, sl] * q_scale).astype(BF16)
    r = _half_rms(ak, float(HEAD_DIM))
    ka_ref[0] = ((ak * (akn_ref[0:1, :] * cos_a) + akp * (akn_ref[1:2, :] * sin_a)) * r).astype(BF16)
    va_ref[0] = av.astype(BF16)
    kb_ref[0] = bk.astype(BF16)
    vb_ref[0] = bv.astype(BF16)

    t1 = t1_ref[...]
    t2 = t2_ref[...]
    cqn = cq * lax.rsqrt(jnp.mean(cq * cq, axis=-1, keepdims=True) + EPS) * cqn_ref[...]
    qu = jnp.dot(cqn.astype(BF16), wq_ref[...], preferred_element_type=F32)
    ckvn = ckv * lax.rsqrt(jnp.mean(ckv * ckv, axis=-1, keepdims=True) + EPS) * ckvn_ref[...]
    kvu = jnp.dot(ckvn.astype(BF16), wkv_ref[...], preferred_element_type=F32)
    k_rope = kr1 * t1 + kr2 * t2
    for hh in range(C_HEADS):
        sl = slice(hh * LANES, (hh + 1) * LANES)
        sl2 = slice(C_Q_W + hh * LANES, C_Q_W + (hh + 1) * LANES)
        qc_ref[0, :, sl] = (qu[:, sl] * t1 + qu[:, sl2] * t2).astype(BF16)
        kc_ref[0, :, sl] = (kvu[:, sl] + k_rope).astype(BF16)
    vc_ref[0] = kvu[:, C_Q_W:].astype(BF16)


def _proj(x, mods, mod_idx, gain, w_attn, aqn, akn, cqn, wq, ckvn, wkv, tabs):
    b, s, d = x.shape
    row = lambda w: pl.BlockSpec((1, ROW_TILE, w), lambda i, j: (i, j, 0))
    tab = pl.BlockSpec((ROW_TILE, LANES), lambda i, j: (j, 0))
    widths = (A_Q_W, A_KV_W, A_KV_W, B_Q_W, B_KV_W, B_KV_W, C_Q_W, C_Q_W, C_V_W)
    return pl.pallas_call(
        _proj_kernel,
        grid=(b, s // ROW_TILE),
        in_specs=[
            row(d),
            pl.BlockSpec((1, 3, 1, d), lambda i, j: (i, mod_idx, 0, 0)),
            _const_spec((1, d)),
            _const_spec((d, PROJ_W)),
            _const_spec((2, LANES)),
            _const_spec((2, LANES)),
            _const_spec((1, C_Q_RANK)),
            _const_spec((C_Q_RANK, 2 * C_Q_W)),
            _const_spec((1, C_KV_RANK)),
            _const_spec((C_KV_RANK, C_Q_W + C_V_W)),
            tab, tab, tab, tab,
        ],
        out_specs=[row(w) for w in widths],
        out_shape=[jax.ShapeDtypeStruct((b, s, w), BF16) for w in widths],
        compiler_params=_params("arbitrary", "arbitrary"),
        name="attn_proj",
    )(x, mods, gain.reshape(1, d), w_attn, aqn, akn, cqn, wq, ckvn, wkv, *tabs)


def _lane_left(shape):
    return lax.broadcasted_iota(jnp.int32, shape, 1) < HEAD_DIM


def _dense_attn_kernel(q_ref, k_ref, v_ref, o_ref, *, plan, exp2_scale):
    left = _lane_left((q_ref.shape[1], LANES))
    for j, pair in enumerate(plan):
        outs = []
        for side, (q_slot, q_half, k_slot, v_slot) in enumerate(pair):
            q = q_ref[0, :, q_slot * LANES:(q_slot + 1) * LANES]
            if q_half is not None:
                q = jnp.where(left if q_half == 0 else jnp.logical_not(left), q, jnp.zeros_like(q))
            k = k_ref[0, :, k_slot * LANES:(k_slot + 1) * LANES]
            s = lax.dot_general(q, k, (((1,), (1,)), ((), ())), preferred_element_type=F32)
            m = jnp.max(s, axis=-1, keepdims=True)
            p = jnp.exp2((s - m) * exp2_scale)
            l = jnp.sum(p, axis=-1, keepdims=True)
            v = v_ref[0, :, v_slot * LANES:(v_slot + 1) * LANES]
            pv = jnp.dot(p.astype(BF16), v, preferred_element_type=F32)
            outs.append(pv * (1.0 / l))
        o_ref[0, :, j * LANES:(j + 1) * LANES] = jnp.where(left, outs[0], outs[1]).astype(BF16)


def _dense_attn(q, k, v, plan, scale, name):
    b, s, qw = q.shape
    ow = len(plan) * LANES
    return pl.pallas_call(
        functools.partial(_dense_attn_kernel, plan=plan, exp2_scale=scale * math.log2(math.e)),
        grid=(b, s // Q_TILE),
        in_specs=[
            pl.BlockSpec((1, Q_TILE, qw), lambda i, j: (i, j, 0)),
            pl.BlockSpec((1, s, k.shape[2]), lambda i, j: (i, 0, 0)),
            pl.BlockSpec((1, s, v.shape[2]), lambda i, j: (i, 0, 0)),
        ],
        out_specs=pl.BlockSpec((1, Q_TILE, ow), lambda i, j: (i, j, 0)),
        out_shape=jax.ShapeDtypeStruct((b, s, ow), BF16),
        compiler_params=_params("arbitrary", "arbitrary"),
        name=name,
    )(q, k, v)


WIN_PLACEMENTS = 3


def _bias_tab_kernel(rb_ref, idx_ref, o_ref):
    for p in range(WIN_PLACEMENTS):
        idx = idx_ref[p]
        for h in range(B_HEADS):
            acc = jnp.full(idx.shape, NEG_INF, F32)
            for bkt in range(NUM_BUCKETS):
                acc = jnp.where(idx == bkt, rb_ref[bkt, h], acc)
            o_ref[p, h] = acc


def _t5_bucket(rel):
    nb = NUM_BUCKETS // 2
    max_exact = nb // 2
    ret = jnp.where(rel > 0, nb, 0)
    n = jnp.abs(rel)
    large = max_exact + (jnp.log(jnp.maximum(n, 1).astype(jnp.float32) / max_exact)
                         / math.log(MAX_DISTANCE / max_exact) * (nb - max_exact)).astype(jnp.int32)
    large = jnp.minimum(large, nb - 1)
    return ret + jnp.where(n < max_exact, n, large)


def _bias_table(rel_bias):
    p = jnp.arange(WIN_PLACEMENTS)[:, None, None]
    r = jnp.arange(Q_TILE)[None, :, None]
    j = jnp.arange(WIN_KEYS)[None, None, :]
    rel = j + p * WINDOW - 2 * WINDOW - r
    idx = jnp.where(jnp.abs(rel) <= WINDOW, _t5_bucket(rel), -1).astype(jnp.int32)
    shape = (WIN_PLACEMENTS, B_HEADS, Q_TILE, WIN_KEYS)
    return pl.pallas_call(
        _bias_tab_kernel,
        in_specs=[pl.BlockSpec(memory_space=pltpu.SMEM),
                  pl.BlockSpec(idx.shape, lambda: (0, 0, 0))],
        out_specs=pl.BlockSpec(shape, lambda: (0, 0, 0, 0)),
        out_shape=jax.ShapeDtypeStruct(shape, F32),
        compiler_params=pltpu.CompilerParams(vmem_limit_bytes=V7X_VMEM_LIMIT_BYTES),
        name="win_bias_table",
    )(rel_bias, idx)


def _win_attn_kernel(sink_ref, q_ref, k_ref, v_ref, tab_ref, o_ref):
    s_len = k_ref.shape[1]
    q0 = pl.program_id(1) * Q_TILE
    start = jnp.clip(q0 - WINDOW, 0, s_len - WIN_KEYS)
    place = (start - q0 + 2 * WINDOW) // WINDOW
    start = pl.multiple_of(start, LANES)
    kw = k_ref[0, pl.ds(start, WIN_KEYS), :]
    vw = v_ref[0, pl.ds(start, WIN_KEYS), :]
    left = _lane_left((Q_TILE, LANES))
    g = B_HEADS // B_KV_HEADS
    for j in range(Q_SLOTS):
        qs = q_ref[0, :, j * LANES:(j + 1) * LANES]
        outs = []
        for side in range(B_KV_HEADS):
            head = side * g + j
            q = jnp.where(left if side == 0 else jnp.logical_not(left), qs, jnp.zeros_like(qs))
            s = lax.dot_general(q, kw, (((1,), (1,)), ((), ())), preferred_element_type=F32)
            s = s + tab_ref[place, head]
            sink = sink_ref[head]
            m = jnp.maximum(jnp.max(s, axis=-1, keepdims=True), sink)
            p = jnp.exp(s - m)
            l = jnp.sum(p, axis=-1, keepdims=True) + jnp.exp(sink - m)
            pv = jnp.dot(p.astype(BF16), vw, preferred_element_type=F32)
            outs.append(pv * (1.0 / l))
        o_ref[0, :, j * LANES:(j + 1) * LANES] = jnp.where(left, outs[0], outs[1]).astype(BF16)


def _win_attn(q, k, v, tab, sink):
    b, s, qw = q.shape
    return pl.pallas_call(
        _win_attn_kernel,
        grid=(b, s // Q_TILE),
        in_specs=[
            pl.BlockSpec(memory_space=pltpu.SMEM),
            pl.BlockSpec((1, Q_TILE, qw), lambda i, j: (i, j, 0)),
            pl.BlockSpec((1, s, k.shape[2]), lambda i, j: (i, 0, 0)),
            pl.BlockSpec((1, s, v.shape[2]), lambda i, j: (i, 0, 0)),
            _const_spec(tab.shape),
        ],
        out_specs=pl.BlockSpec((1, Q_TILE, qw), lambda i, j: (i, j, 0)),
        out_shape=jax.ShapeDtypeStruct((b, s, qw), BF16),
        compiler_params=_params("arbitrary", "arbitrary"),
        name="win_attn",
    )(sink, q, k, v, tab)


def _mix_out_kernel(x_ref, m_ref, g_ref, oa_ref, ob_ref, oc_ref, wg_ref, wa_ref, wb_ref, wc_ref,
                    wo_ref, o_ref):
    x = x_ref[0]
    d = x.shape[-1]
    h = _norm_mod(x, g_ref[...], m_ref[0, 0], m_ref[0, 1]).astype(BF16)
    gates = jnp.dot(h, wg_ref[...], preferred_element_type=F32)
    ya = jnp.dot(oa_ref[0], wa_ref[...], preferred_element_type=F32)
    yb = jnp.dot(ob_ref[0], wb_ref[...], preferred_element_type=F32)
    yc = jnp.dot(oc_ref[0], wc_ref[...], preferred_element_type=F32)
    merged = (_sigmoid(gates[:, :d]) * ya + _sigmoid(gates[:, d:2 * d]) * yb
              + _sigmoid(gates[:, 2 * d:]) * yc)
    out = jnp.dot(merged.astype(BF16), wo_ref[...], preferred_element_type=F32)
    o_ref[0] = x + m_ref[0, 2] * out


def _mix_out(x, mods, mod_idx, gain, o_a, o_b, o_c, w_gates, w_a, w_b, w_c, w_o):
    b, s, d = x.shape
    row = lambda w: pl.BlockSpec((1, ROW_TILE, w), lambda i, j: (i, j, 0))
    return pl.pallas_call(
        _mix_out_kernel,
        grid=(b, s // ROW_TILE),
        in_specs=[
            row(d),
            pl.BlockSpec((1, 3, 1, d), lambda i, j: (i, mod_idx, 0, 0)),
            _const_spec((1, d)),
            row(o_a.shape[2]), row(o_b.shape[2]), row(o_c.shape[2]),
            _const_spec(w_gates.shape), _const_spec(w_a.shape), _const_spec(w_b.shape),
            _const_spec(w_c.shape), _const_spec(w_o.shape),
        ],
        out_specs=row(d),
        out_shape=jax.ShapeDtypeStruct((b, s, d), F32),
        compiler_params=_params("arbitrary", "arbitrary"),
        name="mix_out",
    )(x, mods, gain.reshape(1, d), o_a, o_b, o_c, w_gates, w_a, w_b, w_c, w_o)


def _rope_angles(pos, dim):
    inv = ROPE_THETA ** (-jnp.arange(0, dim, 2, dtype=jnp.float32) / dim)
    ang = pos.astype(jnp.float32)[:, None] * inv[None, :]
    return jnp.cos(ang), jnp.sin(ang)


def _rope_tables(s):
    rows = s // GRID_W
    t = jnp.arange(s)
    row_pos = jnp.repeat(jnp.arange(rows), GRID_W)
    col_pos = jnp.tile(jnp.arange(GRID_W), rows)
    cr, sr = _rope_angles(row_pos, HEAD_DIM // 2)
    cc, sc = _rope_angles(col_pos, HEAD_DIM // 2)
    cos_h = jnp.concatenate([cr, cr, cc, cc], axis=-1)
    sin_h = jnp.concatenate([-sr, sr, -sc, sc], axis=-1)
    reps = LANES // HEAD_DIM
    cos_a = jnp.tile(cos_h, (1, reps))
    sin_a = jnp.tile(sin_h, (1, reps))
    cs, ss = _rope_angles(t, C_ROPE_DIM)
    ones = jnp.ones((s, C_NOPE_DIM), F32)
    zeros_lo = jnp.zeros((s, C_NOPE_DIM), F32)
    zeros_hi = jnp.zeros((s, LANES - C_QK_DIM), F32)
    t1 = jnp.concatenate([ones, cs, cs, zeros_hi], axis=-1)
    t2 = jnp.concatenate([zeros_lo, -ss, ss, zeros_hi], axis=-1)
    return cos_a, sin_a, t1, t2


def _attn_plans():
    plan_a = tuple(((j, 0, 0, 0), (j, 1, 0, 0)) for j in range(Q_SLOTS))
    plan_c = tuple(((2 * j, None, 2 * j, j), (2 * j + 1, None, 2 * j + 1, j))
                   for j in range(C_HEADS // 2))
    return plan_a, plan_c


def kernel(x, c, ada_w, ada_b, norm_ffn1, ffn1_w_gu, ffn1_w_down, norm_mix, w_in, a_q_norm, a_k_norm,
           b_sink, rel_bias, c_q_lat_norm, c_w_q_up, c_kv_lat_norm, c_w_kv_up, w_br_a, w_br_b, w_br_c,
           w_out, norm_ffn2, ffn2_w_gu, ffn2_w_down, final_norm):
    b, s, d = x.shape
    assert d == D_MODEL and s % Q_TILE == 0 and s % ROW_TILE == 0 and s >= WIN_KEYS and s % GRID_W == 0
    depth = ada_w.shape[0]

    mods = _ada_mods(c, ada_w, ada_b)
    tabs = _rope_tables(s)
    bias_tab = _bias_table(rel_bias)
    plan_a, plan_c = _attn_plans()

    proj_cols = _proj_columns()
    q_up_cols = _q_up_columns()
    kv_up_cols = _kv_up_columns()
    head_rows = np.asarray([h * HEAD_DIM + dd for h in _q_head_order() for dd in range(HEAD_DIM)])
    partner = np.arange(HEAD_DIM) ^ (HEAD_DIM // 4)
    reps = LANES // HEAD_DIM

    for l in range(depth):
        x = _ffn(x, mods, 3 * l, norm_ffn1[l], ffn1_w_gu[l], ffn1_w_down[l])

        w_attn = _take_cols(w_in[l], proj_cols).astype(BF16)
        aqn = jnp.stack([jnp.tile(a_q_norm[l], reps), jnp.tile(a_q_norm[l][partner], reps)])
        akn = jnp.stack([jnp.tile(a_k_norm[l], reps), jnp.tile(a_k_norm[l][partner], reps)])
        wq = _take_cols(c_w_q_up[l], q_up_cols).astype(BF16)
        wkv = _take_cols(c_w_kv_up[l], kv_up_cols).astype(BF16)
        qa, ka, va, qb, kb, vb, qc, kc, vc = _proj(
            x, mods, 3 * l + 1, norm_mix[l], w_attn, aqn, akn,
            c_q_lat_norm[l].reshape(1, -1), wq, c_kv_lat_norm[l].reshape(1, -1), wkv, tabs)

        o_a = _dense_attn(qa, ka, va, plan_a, 1.0, "attn_a")
        o_b = _win_attn(qb, kb, vb, bias_tab, b_sink[l])
        o_c = _dense_attn(qc, kc, vc, plan_c, C_QK_DIM ** -0.5, "attn_c")

        x = _mix_out(
            x, mods, 3 * l + 1, norm_mix[l], o_a, o_b, o_c,
            w_in[l][:, GATE_OFF:].astype(BF16),
            w_br_a[l][head_rows].astype(BF16), w_br_b[l][head_rows].astype(BF16),
            w_br_c[l].astype(BF16), w_out[l].astype(BF16))

        x = _ffn(x, mods, 3 * l + 2, norm_ffn2[l], ffn2_w_gu[l], ffn2_w_down[l],
                 final_gain=final_norm if l == depth - 1 else None)
    return x
```

```python
import functools
import math

import numpy as np
import jax
import jax.numpy as jnp
from jax import lax
from jax.experimental import pallas as pl
from jax.experimental.pallas import tpu as pltpu

D_MODEL = 1024
DEPTH = 2
HEAD_DIM = 64
A_HEADS = 6
A_KV_HEADS = 2
B_HEADS = 6
B_KV_HEADS = 2
C_HEADS = 4
C_Q_RANK = 256
C_KV_RANK = 128
C_NOPE_DIM = 64
C_ROPE_DIM = 32
C_V_DIM = 64
C_QK_DIM = C_NOPE_DIM + C_ROPE_DIM
D_FF = 2816
GRID_W = 64
WINDOW = 128
NUM_BUCKETS = 32
MAX_DISTANCE = 128
ROPE_THETA = 10000.0
ADA_CHUNKS = 9
EPS = 1e-6
NEG_INF = -1e30

A_Q_W = A_HEADS * HEAD_DIM
A_KV_W = A_KV_HEADS * HEAD_DIM
B_Q_W = B_HEADS * HEAD_DIM
B_KV_W = B_KV_HEADS * HEAD_DIM
IN_SIZES = (A_Q_W, A_KV_W, A_KV_W, B_Q_W, B_KV_W, B_KV_W,
            C_Q_RANK, C_KV_RANK, C_ROPE_DIM, D_MODEL, D_MODEL, D_MODEL)
IN_OFFS = tuple(int(v) for v in np.cumsum((0,) + IN_SIZES))
GATE_OFF = IN_OFFS[9]

LANES = 128
V7X_VMEM_LIMIT_BYTES = 60000 * 1024

ROW_TILE = 512
Q_TILE = 256
C_Q_TILE = 512
WIN_KEYS = Q_TILE + 2 * WINDOW

BF16 = jnp.bfloat16
F32 = jnp.float32


def _params(*sem):
    return pltpu.CompilerParams(dimension_semantics=sem, vmem_limit_bytes=V7X_VMEM_LIMIT_BYTES)


def _const_spec(shape):
    nd = len(shape)
    return pl.BlockSpec(shape, lambda *_: (0,) * nd, pipeline_mode=pl.Buffered(1))


def _sigmoid(v):
    return 1.0 / (1.0 + jnp.exp(-v))


def _norm_mod(x, gain, shift, scale):
    ms = jnp.mean(x * x, axis=-1, keepdims=True)
    y = x * lax.rsqrt(ms + EPS) * gain
    return y * (1.0 + scale) + shift


def _ada_kernel(c_ref, w_ref, b_ref, o_ref):
    c = c_ref[...]
    cond = (c * _sigmoid(c)).astype(BF16)
    o_ref[0] = jnp.dot(cond, w_ref[0].astype(BF16), preferred_element_type=F32) + b_ref[0]


def _ada_mods(c, ada_w, ada_b):
    depth, d, _ = ada_w.shape
    b = c.shape[0]
    bias = ada_b.reshape(depth * ADA_CHUNKS, 1, d)
    out = pl.pallas_call(
        _ada_kernel,
        grid=(depth, ADA_CHUNKS),
        in_specs=[
            pl.BlockSpec((b, d), lambda l, j: (0, 0)),
            pl.BlockSpec((1, d, d), lambda l, j: (l, 0, j)),
            pl.BlockSpec((1, 1, d), lambda l, j: (l * ADA_CHUNKS + j, 0, 0)),
        ],
        out_specs=pl.BlockSpec((1, b, d), lambda l, j: (l * ADA_CHUNKS + j, 0, 0)),
        out_shape=jax.ShapeDtypeStruct((depth * ADA_CHUNKS, b, d), F32),
        compiler_params=_params("arbitrary", "arbitrary"),
        name="ada_mods",
    )(c, ada_w, bias)
    return jnp.transpose(out, (1, 0, 2))[:, :, None, :]


def _ffn_kernel(x_ref, m_ref, g_ref, wgu_ref, wd_ref, *rest, final):
    o_ref = rest[-1]
    x = x_ref[0]
    h = _norm_mod(x, g_ref[...], m_ref[0, 0], m_ref[0, 1]).astype(BF16)
    gu = jnp.dot(h, wgu_ref[...], preferred_element_type=F32)
    gate = gu[:, :D_FF]
    up = gu[:, D_FF:]
    act = (gate * _sigmoid(gate) * up).astype(BF16)
    y = jnp.dot(act, wd_ref[...], preferred_element_type=F32)
    out = x + (0.5 * m_ref[0, 2]) * y
    if final:
        ms = jnp.mean(out * out, axis=-1, keepdims=True)
        out = out * lax.rsqrt(ms + EPS) * rest[0][...]
    o_ref[0] = out


def _ffn(x, mods, mod_idx, gain, w_gu, w_down, final_gain=None):
    b, s, d = x.shape
    final = final_gain is not None
    in_specs = [
        pl.BlockSpec((1, ROW_TILE, d), lambda i, j: (i, j, 0)),
        pl.BlockSpec((1, 3, 1, d), lambda i, j: (i, mod_idx, 0, 0)),
        _const_spec((1, d)),
        _const_spec((d, 2 * D_FF)),
        _const_spec((D_FF, d)),
    ]
    args = [x, mods, gain.reshape(1, d), w_gu.astype(BF16), w_down.astype(BF16)]
    if final:
        in_specs.append(_const_spec((1, d)))
        args.append(final_gain.reshape(1, d))
    return pl.pallas_call(
        functools.partial(_ffn_kernel, final=final),
        grid=(b, s // ROW_TILE),
        in_specs=in_specs,
        out_specs=pl.BlockSpec((1, ROW_TILE, d), lambda i, j: (i, j, 0)),
        out_shape=jax.ShapeDtypeStruct((b, s, d), F32),
        compiler_params=_params("arbitrary", "arbitrary"),
        name="ffn_final" if final else "ffn",
    )(*args)


_SEG_SIZES = (A_Q_W, A_Q_W, A_KV_W, A_KV_W, A_KV_W, B_Q_W, B_KV_W, B_KV_W,
              C_Q_RANK, C_KV_RANK, LANES, LANES)
_SEG_OFFS = tuple(int(v) for v in np.cumsum((0,) + _SEG_SIZES))
PROJ_W = _SEG_OFFS[-1]
Q_SLOTS = A_HEADS // A_KV_HEADS


def _q_head_order():
    g = A_HEADS // A_KV_HEADS
    return [k * g + j for j in range(g) for k in range(A_KV_HEADS)]


def _proj_columns():
    aq, ak, av, bq, bk, bv, cq, ckv, ckr = IN_OFFS[:9]
    order = _q_head_order()
    dims = np.arange(HEAD_DIM)
    partner = dims ^ (HEAD_DIM // 4)
    cols = []
    cols += [aq + h * HEAD_DIM + d for h in order for d in dims]
    cols += [aq + h * HEAD_DIM + d for h in order for d in partner]
    cols += [ak + h * HEAD_DIM + d for h in range(A_KV_HEADS) for d in dims]
    cols += [ak + h * HEAD_DIM + d for h in range(A_KV_HEADS) for d in partner]
    cols += list(range(av, av + A_KV_W))
    cols += [bq + h * HEAD_DIM + d for h in order for d in dims]
    cols += list(range(bk, bk + B_KV_W))
    cols += list(range(bv, bv + B_KV_W))
    cols += list(range(cq, cq + C_Q_RANK))
    cols += list(range(ckv, ckv + C_KV_RANK))
    rdims = np.arange(C_ROPE_DIM)
    rpartner = rdims ^ (C_ROPE_DIM // 2)
    pad_hi = LANES - C_NOPE_DIM - C_ROPE_DIM
    cols += [-1] * C_NOPE_DIM + [ckr + d for d in rdims] + [-1] * pad_hi
    cols += [-1] * C_NOPE_DIM + [ckr + d for d in rpartner] + [-1] * pad_hi
    cols = np.asarray(cols, np.int32)
    assert cols.shape[0] == PROJ_W
    return cols


def _take_cols(w, cols):
    picked = jnp.take(w, jnp.asarray(np.maximum(cols, 0)), axis=1)
    return jnp.where(jnp.asarray(cols >= 0)[None, :], picked, 0.0)


def _q_up_columns():
    rdims = np.arange(C_ROPE_DIM)
    rpartner = rdims ^ (C_ROPE_DIM // 2)
    pad_hi = LANES - C_QK_DIM
    direct, part = [], []
    for h in range(C_HEADS):
        base = h * C_QK_DIM
        direct += [base + d for d in range(C_NOPE_DIM)] + [base + C_NOPE_DIM + d for d in rdims] + [-1] * pad_hi
        part += [-1] * C_NOPE_DIM + [base + C_NOPE_DIM + d for d in rpartner] + [-1] * pad_hi
    return np.asarray(direct + part, np.int32)


def _kv_up_columns():
    per = C_NOPE_DIM + C_V_DIM
    keys, vals = [], []
    for h in range(C_HEADS):
        keys += [h * per + d for d in range(C_NOPE_DIM)] + [-1] * (LANES - C_NOPE_DIM)
        vals += [h * per + C_NOPE_DIM + d for d in range(C_V_DIM)]
    return np.asarray(keys + vals, np.int32)


C_Q_W = C_HEADS * LANES
C_V_W = C_HEADS * C_V_DIM


def _half_rms(t, n):
    lane = lax.broadcasted_iota(jnp.int32, t.shape, 1)
    left = lane < HEAD_DIM
    sq = t * t
    s_l = jnp.sum(jnp.where(left, sq, 0.0), axis=-1, keepdims=True)
    s_r = jnp.sum(jnp.where(left, 0.0, sq), axis=-1, keepdims=True)
    return jnp.where(left, lax.rsqrt(s_l / n + EPS), lax.rsqrt(s_r / n + EPS))


def _proj_kernel(x_ref, m_ref, g_ref, w_ref, aqn_ref, akn_ref, cqn_ref, wq_ref, ckvn_ref, wkv_ref,
                 ac_ref, as_ref, t1_ref, t2_ref,
                 qa_ref, ka_ref, va_ref, qb_ref, kb_ref, vb_ref, qc_ref, kc_ref, vc_ref):
    x = x_ref[0]
    h = _norm_mod(x, g_ref[...], m_ref[0, 0], m_ref[0, 1]).astype(BF16)
    p = jnp.dot(h, w_ref[...], preferred_element_type=F32)
    seg = [p[:, _SEG_OFFS[i]:_SEG_OFFS[i + 1]] for i in range(len(_SEG_SIZES))]
    aq, aqp, ak, akp, av, bq, bk, bv, cq, ckv, kr1, kr2 = seg

    cos_a = ac_ref[...]
    sin_a = as_ref[...]
    q_scale = HEAD_DIM ** -0.5
    gq_c = aqn_ref[0:1, :] * cos_a
    gq_s = aqn_ref[1:2, :] * sin_a
    for j in range(Q_SLOTS):
        sl = slice(j * LANES, (j + 1) * LANES)
        r = _half_rms(aq[:, sl], float(HEAD_DIM))
        qa_ref[0, :, sl] = ((aq[:, sl] * gq_c + aqp[:, sl] * gq_s) * (r * q_scale)).astype(BF16)
        qb_ref[0, :, sl] = (bq[:, sl] * q_scale).astype(BF16)
    r = _half_rms(ak, float(HEAD_DIM))
    ka_ref[0] = ((ak * (akn_ref[0:1, :] * cos_a) + akp * (akn_ref[1:2, :] * sin_a)) * r).astype(BF16)
    va_ref[0] = av.astype(BF16)
    kb_ref[0] = bk.astype(BF16)
    vb_ref[0] = bv.astype(BF16)

    t1 = t1_ref[...]
    t2 = t2_ref[...]
    cqn = cq * lax.rsqrt(jnp.mean(cq * cq, axis=-1, keepdims=True) + EPS) * cqn_ref[...]
    qu = jnp.dot(cqn.astype(BF16), wq_ref[...], preferred_element_type=F32)
    ckvn = ckv * lax.rsqrt(jnp.mean(ckv * ckv, axis=-1, keepdims=True) + EPS) * ckvn_ref[...]
    kvu = jnp.dot(ckvn.astype(BF16), wkv_ref[...], preferred_element_type=F32)
    k_rope = kr1 * t1 + kr2 * t2
    for hh in range(C_HEADS):
        sl = slice(hh * LANES, (hh + 1) * LANES)
        sl2 = slice(C_Q_W + hh * LANES, C_Q_W + (hh + 1) * LANES)
        qc_ref[0, :, sl] = (qu[:, sl] * t1 + qu[:, sl2] * t2).astype(BF16)
        kc_ref[0, :, sl] = (kvu[:, sl] + k_rope).astype(BF16)
    vc_ref[0] = kvu[:, C_Q_W:].astype(BF16)


def _proj(x, mods, mod_idx, gain, w_attn, aqn, akn, cqn, wq, ckvn, wkv, tabs):
    b, s, d = x.shape
    row = lambda w: pl.BlockSpec((1, ROW_TILE, w), lambda i, j: (i, j, 0))
    tab = pl.BlockSpec((ROW_TILE, LANES), lambda i, j: (j, 0))
    widths = (A_Q_W, A_KV_W, A_KV_W, B_Q_W, B_KV_W, B_KV_W, C_Q_W, C_Q_W, C_V_W)
    return pl.pallas_call(
        _proj_kernel,
        grid=(b, s // ROW_TILE),
        in_specs=[
            row(d),
            pl.BlockSpec((1, 3, 1, d), lambda i, j: (i, mod_idx, 0, 0)),
            _const_spec((1, d)),
            _const_spec((d, PROJ_W)),
            _const_spec((2, LANES)),
            _const_spec((2, LANES)),
            _const_spec((1, C_Q_RANK)),
            _const_spec((C_Q_RANK, 2 * C_Q_W)),
            _const_spec((1, C_KV_RANK)),
            _const_spec((C_KV_RANK, C_Q_W + C_V_W)),
            tab, tab, tab, tab,
        ],
        out_specs=[row(w) for w in widths],
        out_shape=[jax.ShapeDtypeStruct((b, s, w), BF16) for w in widths],
        compiler_params=_params("arbitrary", "arbitrary"),
        name="attn_proj",
    )(x, mods, gain.reshape(1, d), w_attn, aqn, akn, cqn, wq, ckvn, wkv, *tabs)


def _lane_left(shape):
    return lax.broadcasted_iota(jnp.int32, shape, 1) < HEAD_DIM


def _dense_attn_kernel(q_ref, k_ref, v_ref, o_ref, *, plan, exp2_scale):
    tq = q_ref.shape[1]
    left = _lane_left((tq, LANES))
    pieces = {}
    for heads, k_slot, v_slot in plan:
        qs = []
        for q_slot, q_half, _, _ in heads:
            q = q_ref[0, :, q_slot * LANES:(q_slot + 1) * LANES]
            if q_half is not None:
                q = jnp.where(left if q_half == 0 else jnp.logical_not(left), q, jnp.zeros_like(q))
            qs.append(q)
        q = qs[0] if len(qs) == 1 else jnp.concatenate(qs, axis=0)
        k = k_ref[0, :, k_slot * LANES:(k_slot + 1) * LANES]
        s = lax.dot_general(q, k, (((1,), (1,)), ((), ())), preferred_element_type=F32)
        m = jnp.max(s, axis=-1, keepdims=True)
        p = jnp.exp2((s - m) * exp2_scale)
        l = jnp.sum(p, axis=-1, keepdims=True)
        v = v_ref[0, :, v_slot * LANES:(v_slot + 1) * LANES]
        o = jnp.dot(p.astype(BF16), v, preferred_element_type=F32) * (1.0 / l)
        for n, (_, _, out_slot, out_half) in enumerate(heads):
            pieces[(out_slot, out_half)] = o[n * tq:(n + 1) * tq]
    for j in range(o_ref.shape[2] // LANES):
        o_ref[0, :, j * LANES:(j + 1) * LANES] = jnp.where(left, pieces[(j, 0)], pieces[(j, 1)]).astype(BF16)


def _dense_attn(q, k, v, plan, scale, q_tile, name):
    b, s, qw = q.shape
    ow = (max(h[2] for g in plan for h in g[0]) + 1) * LANES
    return pl.pallas_call(
        functools.partial(_dense_attn_kernel, plan=plan, exp2_scale=scale * math.log2(math.e)),
        grid=(b, s // q_tile),
        in_specs=[
            pl.BlockSpec((1, q_tile, qw), lambda i, j: (i, j, 0)),
            pl.BlockSpec((1, s, k.shape[2]), lambda i, j: (i, 0, 0)),
            pl.BlockSpec((1, s, v.shape[2]), lambda i, j: (i, 0, 0)),
        ],
        out_specs=pl.BlockSpec((1, q_tile, ow), lambda i, j: (i, j, 0)),
        out_shape=jax.ShapeDtypeStruct((b, s, ow), BF16),
        compiler_params=_params("arbitrary", "arbitrary"),
        name=name,
    )(q, k, v)


WIN_PLACEMENTS = 3


def _bias_tab_kernel(rb_ref, idx_ref, o_ref):
    for p in range(WIN_PLACEMENTS):
        idx = idx_ref[p]
        for h in range(B_HEADS):
            acc = jnp.full(idx.shape, NEG_INF, F32)
            for bkt in range(NUM_BUCKETS):
                acc = jnp.where(idx == bkt, rb_ref[bkt, h], acc)
            o_ref[p, h] = acc


def _t5_bucket(rel):
    nb = NUM_BUCKETS // 2
    max_exact = nb // 2
    ret = jnp.where(rel > 0, nb, 0)
    n = jnp.abs(rel)
    large = max_exact + (jnp.log(jnp.maximum(n, 1).astype(jnp.float32) / max_exact)
                         / math.log(MAX_DISTANCE / max_exact) * (nb - max_exact)).astype(jnp.int32)
    large = jnp.minimum(large, nb - 1)
    return ret + jnp.where(n < max_exact, n, large)


def _bias_table(rel_bias):
    p = jnp.arange(WIN_PLACEMENTS)[:, None, None]
    r = jnp.arange(Q_TILE)[None, :, None]
    j = jnp.arange(WIN_KEYS)[None, None, :]
    rel = j + p * WINDOW - 2 * WINDOW - r
    idx = jnp.where(jnp.abs(rel) <= WINDOW, _t5_bucket(rel), -1).astype(jnp.int32)
    shape = (WIN_PLACEMENTS, B_HEADS, Q_TILE, WIN_KEYS)
    return pl.pallas_call(
        _bias_tab_kernel,
        in_specs=[pl.BlockSpec(memory_space=pltpu.SMEM),
                  pl.BlockSpec(idx.shape, lambda: (0, 0, 0))],
        out_specs=pl.BlockSpec(shape, lambda: (0, 0, 0, 0)),
        out_shape=jax.ShapeDtypeStruct(shape, F32),
        compiler_params=pltpu.CompilerParams(vmem_limit_bytes=V7X_VMEM_LIMIT_BYTES),
        name="win_bias_table",
    )(rel_bias, idx)


def _win_attn_kernel(sink_ref, q_ref, k_ref, v_ref, tab_ref, o_ref):
    s_len = k_ref.shape[1]
    q0 = pl.program_id(1) * Q_TILE
    start = jnp.clip(q0 - WINDOW, 0, s_len - WIN_KEYS)
    place = (start - q0 + 2 * WINDOW) // WINDOW
    start = pl.multiple_of(start, LANES)
    kw = k_ref[0, pl.ds(start, WIN_KEYS), :]
    vw = v_ref[0, pl.ds(start, WIN_KEYS), :]
    left = _lane_left((Q_TILE, LANES))
    g = B_HEADS // B_KV_HEADS
    slots = [q_ref[0, :, j * LANES:(j + 1) * LANES] for j in range(Q_SLOTS)]
    halves = []
    for side in range(B_KV_HEADS):
        keep = left if side == 0 else jnp.logical_not(left)
        q = jnp.concatenate([jnp.where(keep, qs, jnp.zeros_like(qs)) for qs in slots], axis=0)
        s = lax.dot_general(q, kw, (((1,), (1,)), ((), ())), preferred_element_type=F32)
        outs = []
        for j in range(g):
            head = side * g + j
            sj = s[j * Q_TILE:(j + 1) * Q_TILE] + tab_ref[place, head]
            sink = sink_ref[head]
            m = jnp.maximum(jnp.max(sj, axis=-1, keepdims=True), sink)
            p = jnp.exp(sj - m)
            l = jnp.sum(p, axis=-1, keepdims=True) + jnp.exp(sink - m)
            outs.append((p.astype(BF16), 1.0 / l))
        pv = jnp.dot(jnp.concatenate([o[0] for o in outs], axis=0), vw, preferred_element_type=F32)
        halves.append([pv[j * Q_TILE:(j + 1) * Q_TILE] * outs[j][1] for j in range(g)])
    for j in range(Q_SLOTS):
        o_ref[0, :, j * LANES:(j + 1) * LANES] = jnp.where(left, halves[0][j], halves[1][j]).astype(BF16)


def _win_attn(q, k, v, tab, sink):
    b, s, qw = q.shape
    return pl.pallas_call(
        _win_attn_kernel,
        grid=(b, s // Q_TILE),
        in_specs=[
            pl.BlockSpec(memory_space=pltpu.SMEM),
            pl.BlockSpec((1, Q_TILE, qw), lambda i, j: (i, j, 0)),
            pl.BlockSpec((1, s, k.shape[2]), lambda i, j: (i, 0, 0)),
            pl.BlockSpec((1, s, v.shape[2]), lambda i, j: (i, 0, 0)),
            _const_spec(tab.shape),
        ],
        out_specs=pl.BlockSpec((1, Q_TILE, qw), lambda i, j: (i, j, 0)),
        out_shape=jax.ShapeDtypeStruct((b, s, qw), BF16),
        compiler_params=_params("arbitrary", "arbitrary"),
        name="win_attn",
    )(sink, q, k, v, tab)


def _mix_out_kernel(x_ref, m_ref, g_ref, oa_ref, ob_ref, oc_ref, wg_ref, wa_ref, wb_ref, wc_ref,
                    wo_ref, o_ref):
    x = x_ref[0]
    d = x.shape[-1]
    h = _norm_mod(x, g_ref[...], m_ref[0, 0], m_ref[0, 1]).astype(BF16)
    gates = jnp.dot(h, wg_ref[...], preferred_element_type=F32)
    ya = jnp.dot(oa_ref[0], wa_ref[...], preferred_element_type=F32)
    yb = jnp.dot(ob_ref[0], wb_ref[...], preferred_element_type=F32)
    yc = jnp.dot(oc_ref[0], wc_ref[...], preferred_element_type=F32)
    merged = (_sigmoid(gates[:, :d]) * ya + _sigmoid(gates[:, d:2 * d]) * yb
              + _sigmoid(gates[:, 2 * d:]) * yc)
    out = jnp.dot(merged.astype(BF16), wo_ref[...], preferred_element_type=F32)
    o_ref[0] = x + m_ref[0, 2] * out


def _mix_out(x, mods, mod_idx, gain, o_a, o_b, o_c, w_gates, w_a, w_b, w_c, w_o):
    b, s, d = x.shape
    row = lambda w: pl.BlockSpec((1, ROW_TILE, w), lambda i, j: (i, j, 0))
    return pl.pallas_call(
        _mix_out_kernel,
        grid=(b, s // ROW_TILE),
        in_specs=[
            row(d),
            pl.BlockSpec((1, 3, 1, d), lambda i, j: (i, mod_idx, 0, 0)),
            _const_spec((1, d)),
            row(o_a.shape[2]), row(o_b.shape[2]), row(o_c.shape[2]),
            _const_spec(w_gates.shape), _const_spec(w_a.shape), _const_spec(w_b.shape),
            _const_spec(w_c.shape), _const_spec(w_o.shape),
        ],
        out_specs=row(d),
        out_shape=jax.ShapeDtypeStruct((b, s, d), F32),
        compiler_params=_params("arbitrary", "arbitrary"),
        name="mix_out",
    )(x, mods, gain.reshape(1, d), o_a, o_b, o_c, w_gates, w_a, w_b, w_c, w_o)


def _rope_angles(pos, dim):
    inv = ROPE_THETA ** (-jnp.arange(0, dim, 2, dtype=jnp.float32) / dim)
    ang = pos.astype(jnp.float32)[:, None] * inv[None, :]
    return jnp.cos(ang), jnp.sin(ang)


def _rope_tables(s):
    rows = s // GRID_W
    t = jnp.arange(s)
    row_pos = jnp.repeat(jnp.arange(rows), GRID_W)
    col_pos = jnp.tile(jnp.arange(GRID_W), rows)
    cr, sr = _rope_angles(row_pos, HEAD_DIM // 2)
    cc, sc = _rope_angles(col_pos, HEAD_DIM // 2)
    cos_h = jnp.concatenate([cr, cr, cc, cc], axis=-1)
    sin_h = jnp.concatenate([-sr, sr, -sc, sc], axis=-1)
    reps = LANES // HEAD_DIM
    cos_a = jnp.tile(cos_h, (1, reps))
    sin_a = jnp.tile(sin_h, (1, reps))
    cs, ss = _rope_angles(t, C_ROPE_DIM)
    ones = jnp.ones((s, C_NOPE_DIM), F32)
    zeros_lo = jnp.zeros((s, C_NOPE_DIM), F32)
    zeros_hi = jnp.zeros((s, LANES - C_QK_DIM), F32)
    t1 = jnp.concatenate([ones, cs, cs, zeros_hi], axis=-1)
    t2 = jnp.concatenate([zeros_lo, -ss, ss, zeros_hi], axis=-1)
    return cos_a, sin_a, t1, t2


def _attn_plans():
    plan_a = tuple((((j, g, j, g),), 0, 0) for j in range(Q_SLOTS) for g in range(A_KV_HEADS))
    plan_c = tuple((((h, None, h // 2, h % 2),), h, h // 2) for h in range(C_HEADS))
    return plan_a, plan_c


def kernel(x, c, ada_w, ada_b, norm_ffn1, ffn1_w_gu, ffn1_w_down, norm_mix, w_in, a_q_norm, a_k_norm,
           b_sink, rel_bias, c_q_lat_norm, c_w_q_up, c_kv_lat_norm, c_w_kv_up, w_br_a, w_br_b, w_br_c,
           w_out, norm_ffn2, ffn2_w_gu, ffn2_w_down, final_norm):
    b, s, d = x.shape
    assert d == D_MODEL and s % max(Q_TILE, C_Q_TILE, ROW_TILE) == 0 and s >= WIN_KEYS and s % GRID_W == 0
    depth = ada_w.shape[0]

    mods = _ada_mods(c, ada_w, ada_b)
    tabs = _rope_tables(s)
    bias_tab = _bias_table(rel_bias)
    plan_a, plan_c = _attn_plans()

    proj_cols = _proj_columns()
    q_up_cols = _q_up_columns()
    kv_up_cols = _kv_up_columns()
    head_rows = np.asarray([h * HEAD_DIM + dd for h in _q_head_order() for dd in range(HEAD_DIM)])
    partner = np.arange(HEAD_DIM) ^ (HEAD_DIM // 4)
    reps = LANES // HEAD_DIM

    for l in range(depth):
        x = _ffn(x, mods, 3 * l, norm_ffn1[l], ffn1_w_gu[l], ffn1_w_down[l])

        w_attn = _take_cols(w_in[l], proj_cols).astype(BF16)
        aqn = jnp.stack([jnp.tile(a_q_norm[l], reps), jnp.tile(a_q_norm[l][partner], reps)])
        akn = jnp.stack([jnp.tile(a_k_norm[l], reps), jnp.tile(a_k_norm[l][partner], reps)])
        wq = _take_cols(c_w_q_up[l], q_up_cols).astype(BF16)
        wkv = _take_cols(c_w_kv_up[l], kv_up_cols).astype(BF16)
        qa, ka, va, qb, kb, vb, qc, kc, vc = _proj(
            x, mods, 3 * l + 1, norm_mix[l], w_attn, aqn, akn,
            c_q_lat_norm[l].reshape(1, -1), wq, c_kv_lat_norm[l].reshape(1, -1), wkv, tabs)

        o_a = _dense_attn(qa, ka, va, plan_a, 1.0, Q_TILE, "attn_a")
        o_b = _win_attn(qb, kb, vb, bias_tab, b_sink[l])
        o_c = _dense_attn(qc, kc, vc, plan_c, C_QK_DIM ** -0.5, C_Q_TILE, "attn_c")

        x = _mix_out(
            x, mods, 3 * l + 1, norm_mix[l], o_a, o_b, o_c,
            w_in[l][:, GATE_OFF:].astype(BF16),
            w_br_a[l][head_rows].astype(BF16), w_br_b[l][head_rows].astype(BF16),
            w_br_c[l].astype(BF16), w_out[l].astype(BF16))

        x = _ffn(x, mods, 3 * l + 2, norm_ffn2[l], ffn2_w_gu[l], ffn2_w_down[l],
                 final_gain=final_norm if l == depth - 1 else None)
    return x
```

```python
import functools
import math

import numpy as np
import jax
import jax.numpy as jnp
from jax import lax
from jax.experimental import pallas as pl
from jax.experimental.pallas import tpu as pltpu

D_MODEL = 1024
DEPTH = 2
HEAD_DIM = 64
A_HEADS = 6
A_KV_HEADS = 2
B_HEADS = 6
B_KV_HEADS = 2
C_HEADS = 4
C_Q_RANK = 256
C_KV_RANK = 128
C_NOPE_DIM = 64
C_ROPE_DIM = 32
C_V_DIM = 64
C_QK_DIM = C_NOPE_DIM + C_ROPE_DIM
D_FF = 2816
GRID_W = 64
WINDOW = 128
NUM_BUCKETS = 32
MAX_DISTANCE = 128
ROPE_THETA = 10000.0
ADA_CHUNKS = 9
EPS = 1e-6
NEG_INF = -1e30

A_Q_W = A_HEADS * HEAD_DIM
A_KV_W = A_KV_HEADS * HEAD_DIM
B_Q_W = B_HEADS * HEAD_DIM
B_KV_W = B_KV_HEADS * HEAD_DIM
IN_SIZES = (A_Q_W, A_KV_W, A_KV_W, B_Q_W, B_KV_W, B_KV_W,
            C_Q_RANK, C_KV_RANK, C_ROPE_DIM, D_MODEL, D_MODEL, D_MODEL)
IN_OFFS = tuple(int(v) for v in np.cumsum((0,) + IN_SIZES))
GATE_OFF = IN_OFFS[9]

LANES = 128
V_SLOT_W = 2 * LANES
LOG2_E = math.log2(math.e)
V7X_VMEM_LIMIT_BYTES = 60000 * 1024

ROW_TILE = 512
Q_TILE = 256
C_Q_TILE = 256
KEY_PIECE = 512
WIN_KEYS = Q_TILE + 2 * WINDOW

BF16 = jnp.bfloat16
F32 = jnp.float32


def _params(*sem):
    return pltpu.CompilerParams(dimension_semantics=sem, vmem_limit_bytes=V7X_VMEM_LIMIT_BYTES)


def _const_spec(shape):
    nd = len(shape)
    return pl.BlockSpec(shape, lambda *_: (0,) * nd, pipeline_mode=pl.Buffered(1))


def _sigmoid(v):
    return 1.0 / (1.0 + jnp.exp(-v))


def _norm_mod(x, gain, shift, scale):
    ms = jnp.mean(x * x, axis=-1, keepdims=True)
    y = x * lax.rsqrt(ms + EPS) * gain
    return y * (1.0 + scale) + shift


def _ada_kernel(c_ref, w_ref, b_ref, o_ref):
    c = c_ref[...]
    cond = (c * _sigmoid(c)).astype(BF16)
    o_ref[0] = jnp.dot(cond, w_ref[0].astype(BF16), preferred_element_type=F32) + b_ref[0]


def _ada_mods(c, ada_w, ada_b):
    depth, d, _ = ada_w.shape
    b = c.shape[0]
    bias = ada_b.reshape(depth * ADA_CHUNKS, 1, d)
    out = pl.pallas_call(
        _ada_kernel,
        grid=(depth, ADA_CHUNKS),
        in_specs=[
            pl.BlockSpec((b, d), lambda l, j: (0, 0)),
            pl.BlockSpec((1, d, d), lambda l, j: (l, 0, j)),
            pl.BlockSpec((1, 1, d), lambda l, j: (l * ADA_CHUNKS + j, 0, 0)),
        ],
        out_specs=pl.BlockSpec((1, b, d), lambda l, j: (l * ADA_CHUNKS + j, 0, 0)),
        out_shape=jax.ShapeDtypeStruct((depth * ADA_CHUNKS, b, d), F32),
        compiler_params=_params("arbitrary", "arbitrary"),
        name="ada_mods",
    )(c, ada_w, bias)
    return jnp.transpose(out, (1, 0, 2))[:, :, None, :]


def _ffn_kernel(x_ref, m_ref, g_ref, wgu_ref, wd_ref, *rest, final):
    o_ref = rest[-1]
    x = x_ref[0]
    h = _norm_mod(x, g_ref[...], m_ref[0, 0], m_ref[0, 1]).astype(BF16)
    gu = jnp.dot(h, wgu_ref[...], preferred_element_type=F32)
    gate = gu[:, :D_FF]
    up = gu[:, D_FF:]
    act = (gate * _sigmoid(gate) * up).astype(BF16)
    y = jnp.dot(act, wd_ref[...], preferred_element_type=F32)
    out = x + (0.5 * m_ref[0, 2]) * y
    if final:
        ms = jnp.mean(out * out, axis=-1, keepdims=True)
        out = out * lax.rsqrt(ms + EPS) * rest[0][...]
    o_ref[0] = out


def _ffn(x, mods, mod_idx, gain, w_gu, w_down, final_gain=None):
    b, s, d = x.shape
    final = final_gain is not None
    in_specs = [
        pl.BlockSpec((1, ROW_TILE, d), lambda i, j: (i, j, 0)),
        pl.BlockSpec((1, 3, 1, d), lambda i, j: (i, mod_idx, 0, 0)),
        _const_spec((1, d)),
        _const_spec((d, 2 * D_FF)),
        _const_spec((D_FF, d)),
    ]
    args = [x, mods, gain.reshape(1, d), w_gu.astype(BF16), w_down.astype(BF16)]
    if final:
        in_specs.append(_const_spec((1, d)))
        args.append(final_gain.reshape(1, d))
    return pl.pallas_call(
        functools.partial(_ffn_kernel, final=final),
        grid=(b, s // ROW_TILE),
        in_specs=in_specs,
        out_specs=pl.BlockSpec((1, ROW_TILE, d), lambda i, j: (i, j, 0)),
        out_shape=jax.ShapeDtypeStruct((b, s, d), F32),
        compiler_params=_params("arbitrary", "arbitrary"),
        name="ffn_final" if final else "ffn",
    )(*args)


_SEG_SIZES = (A_Q_W, A_Q_W, A_KV_W, A_KV_W, A_KV_W, B_Q_W, B_KV_W, B_KV_W,
              C_Q_RANK, C_KV_RANK, LANES, LANES)
_SEG_OFFS = tuple(int(v) for v in np.cumsum((0,) + _SEG_SIZES))
PROJ_W = _SEG_OFFS[-1]
Q_SLOTS = A_HEADS // A_KV_HEADS


def _q_head_order():
    g = A_HEADS // A_KV_HEADS
    return [k * g + j for j in range(g) for k in range(A_KV_HEADS)]


def _proj_columns():
    aq, ak, av, bq, bk, bv, cq, ckv, ckr = IN_OFFS[:9]
    order = _q_head_order()
    dims = np.arange(HEAD_DIM)
    partner = dims ^ (HEAD_DIM // 4)
    cols = []
    cols += [aq + h * HEAD_DIM + d for h in order for d in dims]
    cols += [aq + h * HEAD_DIM + d for h in order for d in partner]
    cols += [ak + h * HEAD_DIM + d for h in range(A_KV_HEADS) for d in dims]
    cols += [ak + h * HEAD_DIM + d for h in range(A_KV_HEADS) for d in partner]
    cols += list(range(av, av + A_KV_W))
    cols += [bq + h * HEAD_DIM + d for h in order for d in dims]
    cols += list(range(bk, bk + B_KV_W))
    cols += list(range(bv, bv + B_KV_W))
    cols += list(range(cq, cq + C_Q_RANK))
    cols += list(range(ckv, ckv + C_KV_RANK))
    rdims = np.arange(C_ROPE_DIM)
    rpartner = rdims ^ (C_ROPE_DIM // 2)
    pad_hi = LANES - C_NOPE_DIM - C_ROPE_DIM
    cols += [-1] * C_NOPE_DIM + [ckr + d for d in rdims] + [-1] * pad_hi
    cols += [-1] * C_NOPE_DIM + [ckr + d for d in rpartner] + [-1] * pad_hi
    cols = np.asarray(cols, np.int32)
    assert cols.shape[0] == PROJ_W
    return cols


def _take_cols(w, cols):
    picked = jnp.take(w, jnp.asarray(np.maximum(cols, 0)), axis=1)
    return jnp.where(jnp.asarray(cols >= 0)[None, :], picked, 0.0)


def _q_up_columns():
    rdims = np.arange(C_ROPE_DIM)
    rpartner = rdims ^ (C_ROPE_DIM // 2)
    pad_hi = LANES - C_QK_DIM
    direct, part = [], []
    for h in range(C_HEADS):
        base = h * C_QK_DIM
        direct += [base + d for d in range(C_NOPE_DIM)] + [base + C_NOPE_DIM + d for d in rdims] + [-1] * pad_hi
        part += [-1] * C_NOPE_DIM + [base + C_NOPE_DIM + d for d in rpartner] + [-1] * pad_hi
    return np.asarray(direct + part, np.int32)


def _kv_up_columns():
    per = C_NOPE_DIM + C_V_DIM
    keys, vals = [], []
    for h in range(C_HEADS):
        keys += [h * per + d for d in range(C_NOPE_DIM)] + [-1] * (LANES - C_NOPE_DIM)
        vals += [h * per + C_NOPE_DIM + d for d in range(C_V_DIM)]
    return np.asarray(keys + vals, np.int32)


C_Q_W = C_HEADS * LANES
C_V_W = C_HEADS * C_V_DIM


def _half_rms(t, n):
    lane = lax.broadcasted_iota(jnp.int32, t.shape, 1)
    left = lane < HEAD_DIM
    sq = t * t
    s_l = jnp.sum(jnp.where(left, sq, 0.0), axis=-1, keepdims=True)
    s_r = jnp.sum(jnp.where(left, 0.0, sq), axis=-1, keepdims=True)
    return jnp.where(left, lax.rsqrt(s_l / n + EPS), lax.rsqrt(s_r / n + EPS))


def _proj_kernel(x_ref, m_ref, g_ref, w_ref, aqn_ref, akn_ref, cqn_ref, wq_ref, ckvn_ref, wkv_ref,
                 ac_ref, as_ref, t1_ref, t2_ref,
                 qa_ref, ka_ref, va_ref, qb_ref, kb_ref, vb_ref, qc_ref, kc_ref, vc_ref):
    x = x_ref[0]
    h = _norm_mod(x, g_ref[...], m_ref[0, 0], m_ref[0, 1]).astype(BF16)
    p = jnp.dot(h, w_ref[...], preferred_element_type=F32)
    seg = [p[:, _SEG_OFFS[i]:_SEG_OFFS[i + 1]] for i in range(len(_SEG_SIZES))]
    aq, aqp, ak, akp, av, bq, bk, bv, cq, ckv, kr1, kr2 = seg

    cos_a = ac_ref[...]
    sin_a = as_ref[...]
    qa_scale = HEAD_DIM ** -0.5 * LOG2_E
    qb_scale = HEAD_DIM ** -0.5
    qc_scale = C_QK_DIM ** -0.5 * LOG2_E
    ones = jnp.ones((x.shape[0], LANES), BF16)
    gq_c = aqn_ref[0:1, :] * cos_a
    gq_s = aqn_ref[1:2, :] * sin_a
    for j in range(Q_SLOTS):
        sl = slice(j * LANES, (j + 1) * LANES)
        r = _half_rms(aq[:, sl], float(HEAD_DIM))
        qa_ref[0, :, sl] = ((aq[:, sl] * gq_c + aqp[:, sl] * gq_s) * (r * qa_scale)).astype(BF16)
        qb_ref[0, :, sl] = (bq[:, sl] * qb_scale).astype(BF16)
    r = _half_rms(ak, float(HEAD_DIM))
    ka_ref[0] = ((ak * (akn_ref[0:1, :] * cos_a) + akp * (akn_ref[1:2, :] * sin_a)) * r).astype(BF16)
    va_ref[0, :, :LANES] = av.astype(BF16)
    va_ref[0, :, LANES:] = ones
    kb_ref[0] = bk.astype(BF16)
    vb_ref[0] = bv.astype(BF16)

    t1 = t1_ref[...]
    t2 = t2_ref[...]
    cqn = cq * lax.rsqrt(jnp.mean(cq * cq, axis=-1, keepdims=True) + EPS) * cqn_ref[...]
    qu = jnp.dot(cqn.astype(BF16), wq_ref[...], preferred_element_type=F32)
    ckvn = ckv * lax.rsqrt(jnp.mean(ckv * ckv, axis=-1, keepdims=True) + EPS) * ckvn_ref[...]
    kvu = jnp.dot(ckvn.astype(BF16), wkv_ref[...], preferred_element_type=F32)
    k_rope = kr1 * t1 + kr2 * t2
    for hh in range(C_HEADS):
        sl = slice(hh * LANES, (hh + 1) * LANES)
        sl2 = slice(C_Q_W + hh * LANES, C_Q_W + (hh + 1) * LANES)
        qc_ref[0, :, sl] = ((qu[:, sl] * t1 + qu[:, sl2] * t2) * qc_scale).astype(BF16)
        kc_ref[0, :, sl] = (kvu[:, sl] + k_rope).astype(BF16)
    for j in range(C_HEADS // 2):
        vc_ref[0, :, j * V_SLOT_W:j * V_SLOT_W + LANES] = kvu[:, C_Q_W + j * LANES:C_Q_W + (j + 1) * LANES].astype(BF16)
        vc_ref[0, :, j * V_SLOT_W + LANES:(j + 1) * V_SLOT_W] = ones


def _proj(x, mods, mod_idx, gain, w_attn, aqn, akn, cqn, wq, ckvn, wkv, tabs):
    b, s, d = x.shape
    row = lambda w: pl.BlockSpec((1, ROW_TILE, w), lambda i, j: (i, j, 0))
    tab = pl.BlockSpec((ROW_TILE, LANES), lambda i, j: (j, 0))
    widths = (A_Q_W, A_KV_W, V_SLOT_W, B_Q_W, B_KV_W, B_KV_W, C_Q_W, C_Q_W, C_HEADS // 2 * V_SLOT_W)
    return pl.pallas_call(
        _proj_kernel,
        grid=(b, s // ROW_TILE),
        in_specs=[
            row(d),
            pl.BlockSpec((1, 3, 1, d), lambda i, j: (i, mod_idx, 0, 0)),
            _const_spec((1, d)),
            _const_spec((d, PROJ_W)),
            _const_spec((2, LANES)),
            _const_spec((2, LANES)),
            _const_spec((1, C_Q_RANK)),
            _const_spec((C_Q_RANK, 2 * C_Q_W)),
            _const_spec((1, C_KV_RANK)),
            _const_spec((C_KV_RANK, C_Q_W + C_V_W)),
            tab, tab, tab, tab,
        ],
        out_specs=[row(w) for w in widths],
        out_shape=[jax.ShapeDtypeStruct((b, s, w), BF16) for w in widths],
        compiler_params=_params("arbitrary", "arbitrary"),
        name="attn_proj",
    )(x, mods, gain.reshape(1, d), w_attn, aqn, akn, cqn, wq, ckvn, wkv, *tabs)


def _lane_left(shape):
    return lax.broadcasted_iota(jnp.int32, shape, 1) < HEAD_DIM


def _dense_attn_steps(q_ref, k_ref, v_ref, o_ref, s_new, m_new, s_old, m_old, *, plan):
    tq = q_ref.shape[1]
    n_keys = k_ref.shape[1]
    left = _lane_left((tq, LANES))
    row = 0
    pieces = {}
    for heads, k_slot, v_slot in plan:
        qs = []
        for q_slot, q_half, _, _ in heads:
            q = q_ref[0, :, q_slot * LANES:(q_slot + 1) * LANES]
            if q_half is not None:
                q = jnp.where(left if q_half == 0 else jnp.logical_not(left), q, jnp.zeros_like(q))
            qs.append(q)
        q = qs[0] if len(qs) == 1 else jnp.concatenate(qs, axis=0)
        rows = slice(row, row + q.shape[0])
        row += q.shape[0]
        m_prev = m_old[rows, :]
        m_run = o_run = None
        for c in range(n_keys // KEY_PIECE):
            cols = slice(c * KEY_PIECE, (c + 1) * KEY_PIECE)
            k = k_ref[0, cols, k_slot * LANES:(k_slot + 1) * LANES]
            s = lax.dot_general(q, k, (((1,), (1,)), ((), ())), preferred_element_type=F32)
            s_new[rows, cols] = s
            mx = jnp.max(s, axis=-1, keepdims=True)
            m_run = mx if c == 0 else jnp.maximum(m_run, mx)

            p = jnp.exp2(s_old[rows, cols] - m_prev)
            v = v_ref[0, cols, v_slot * V_SLOT_W:(v_slot + 1) * V_SLOT_W]
            pv = jnp.dot(p.astype(BF16), v, preferred_element_type=F32)
            o_run = pv if c == 0 else o_run + pv
        m_new[rows, :] = m_run
        o = o_run[:, :LANES] * (1.0 / o_run[:, LANES:])
        for n, (_, _, out_slot, out_half) in enumerate(heads):
            pieces[(out_slot, out_half)] = o[n * tq:(n + 1) * tq]
    for j in range(o_ref.shape[2] // LANES):
        o_ref[0, :, j * LANES:(j + 1) * LANES] = jnp.where(left, pieces[(j, 0)], pieces[(j, 1)]).astype(BF16)


def _dense_attn_kernel(q_ref, k_ref, v_ref, o_ref, s_a, m_a, s_b, m_b, *, plan):
    t = pl.program_id(0)
    step = functools.partial(_dense_attn_steps, q_ref, k_ref, v_ref, o_ref, plan=plan)

    @pl.when(t == 0)
    def _():
        s_b[...] = jnp.zeros(s_b.shape, F32)
        m_b[...] = jnp.zeros(m_b.shape, F32)

    @pl.when(t % 2 == 0)
    def _():
        step(s_a, m_a, s_b, m_b)

    @pl.when(t % 2 == 1)
    def _():
        step(s_b, m_b, s_a, m_a)


def _dense_attn(q, k, v, plan, q_tile, name):
    b, s, qw = q.shape
    ow = (max(h[2] for g in plan for h in g[0]) + 1) * LANES
    nq = s // q_tile
    tiles = b * nq
    stacked_rows = sum(len(g[0]) for g in plan) * q_tile

    def cur(t):
        return jnp.minimum(t, tiles - 1)

    def prev(t):
        return jnp.maximum(t - 1, 0)

    return pl.pallas_call(
        functools.partial(_dense_attn_kernel, plan=plan),
        grid=(tiles + 1,),
        in_specs=[
            pl.BlockSpec((1, q_tile, qw), lambda t: (cur(t) // nq, cur(t) % nq, 0)),
            pl.BlockSpec((1, s, k.shape[2]), lambda t: (cur(t) // nq, 0, 0)),
            pl.BlockSpec((1, s, v.shape[2]), lambda t: (prev(t) // nq, 0, 0)),
        ],
        out_specs=pl.BlockSpec((1, q_tile, ow), lambda t: (prev(t) // nq, prev(t) % nq, 0)),
        out_shape=jax.ShapeDtypeStruct((b, s, ow), BF16),
        scratch_shapes=[pltpu.VMEM((stacked_rows, s), F32), pltpu.VMEM((stacked_rows, 1), F32),
                        pltpu.VMEM((stacked_rows, s), F32), pltpu.VMEM((stacked_rows, 1), F32)],
        compiler_params=_params("arbitrary"),
        name=name,
    )(q, k, v)


WIN_PLACEMENTS = 3


def _bias_tab_kernel(rb_ref, idx_ref, o_ref):
    for p in range(WIN_PLACEMENTS):
        idx = idx_ref[p]
        for h in range(B_HEADS):
            acc = jnp.full(idx.shape, NEG_INF, F32)
            for bkt in range(NUM_BUCKETS):
                acc = jnp.where(idx == bkt, rb_ref[bkt, h], acc)
            o_ref[p, h] = acc


def _t5_bucket(rel):
    nb = NUM_BUCKETS // 2
    max_exact = nb // 2
    ret = jnp.where(rel > 0, nb, 0)
    n = jnp.abs(rel)
    large = max_exact + (jnp.log(jnp.maximum(n, 1).astype(jnp.float32) / max_exact)
                         / math.log(MAX_DISTANCE / max_exact) * (nb - max_exact)).astype(jnp.int32)
    large = jnp.minimum(large, nb - 1)
    return ret + jnp.where(n < max_exact, n, large)


def _bias_table(rel_bias):
    p = jnp.arange(WIN_PLACEMENTS)[:, None, None]
    r = jnp.arange(Q_TILE)[None, :, None]
    j = jnp.arange(WIN_KEYS)[None, None, :]
    rel = j + p * WINDOW - 2 * WINDOW - r
    idx = jnp.where(jnp.abs(rel) <= WINDOW, _t5_bucket(rel), -1).astype(jnp.int32)
    shape = (WIN_PLACEMENTS, B_HEADS, Q_TILE, WIN_KEYS)
    return pl.pallas_call(
        _bias_tab_kernel,
        in_specs=[pl.BlockSpec(memory_space=pltpu.SMEM),
                  pl.BlockSpec(idx.shape, lambda: (0, 0, 0))],
        out_specs=pl.BlockSpec(shape, lambda: (0, 0, 0, 0)),
        out_shape=jax.ShapeDtypeStruct(shape, F32),
        compiler_params=pltpu.CompilerParams(vmem_limit_bytes=V7X_VMEM_LIMIT_BYTES),
        name="win_bias_table",
    )(rel_bias, idx)


def _win_attn_kernel(sink_ref, q_ref, k_ref, v_ref, tab_ref, o_ref):
    s_len = k_ref.shape[1]
    q0 = pl.program_id(1) * Q_TILE
    start = jnp.clip(q0 - WINDOW, 0, s_len - WIN_KEYS)
    place = (start - q0 + 2 * WINDOW) // WINDOW
    start = pl.multiple_of(start, LANES)
    kw = k_ref[0, pl.ds(start, WIN_KEYS), :]
    vw = v_ref[0, pl.ds(start, WIN_KEYS), :]
    left = _lane_left((Q_TILE, LANES))
    g = B_HEADS // B_KV_HEADS
    slots = [q_ref[0, :, j * LANES:(j + 1) * LANES] for j in range(Q_SLOTS)]
    halves = []
    for side in range(B_KV_HEADS):
        keep = left if side == 0 else jnp.logical_not(left)
        q = jnp.concatenate([jnp.where(keep, qs, jnp.zeros_like(qs)) for qs in slots], axis=0)
        s = lax.dot_general(q, kw, (((1,), (1,)), ((), ())), preferred_element_type=F32)
        outs = []
        for j in range(g):
            head = side * g + j
            sj = s[j * Q_TILE:(j + 1) * Q_TILE] + tab_ref[place, head]
            sink = sink_ref[head]
            m = jnp.maximum(jnp.max(sj, axis=-1, keepdims=True), sink)
            p = jnp.exp(sj - m)
            l = jnp.sum(p, axis=-1, keepdims=True) + jnp.exp(sink - m)
            outs.append((p.astype(BF16), 1.0 / l))
        pv = jnp.dot(jnp.concatenate([o[0] for o in outs], axis=0), vw, preferred_element_type=F32)
        halves.append([pv[j * Q_TILE:(j + 1) * Q_TILE] * outs[j][1] for j in range(g)])
    for j in range(Q_SLOTS):
        o_ref[0, :, j * LANES:(j + 1) * LANES] = jnp.where(left, halves[0][j], halves[1][j]).astype(BF16)


def _win_attn(q, k, v, tab, sink):
    b, s, qw = q.shape
    return pl.pallas_call(
        _win_attn_kernel,
        grid=(b, s // Q_TILE),
        in_specs=[
            pl.BlockSpec(memory_space=pltpu.SMEM),
            pl.BlockSpec((1, Q_TILE, qw), lambda i, j: (i, j, 0)),
            pl.BlockSpec((1, s, k.shape[2]), lambda i, j: (i, 0, 0)),
            pl.BlockSpec((1, s, v.shape[2]), lambda i, j: (i, 0, 0)),
            _const_spec(tab.shape),
        ],
        out_specs=pl.BlockSpec((1, Q_TILE, qw), lambda i, j: (i, j, 0)),
        out_shape=jax.ShapeDtypeStruct((b, s, qw), BF16),
        compiler_params=_params("arbitrary", "arbitrary"),
        name="win_attn",
    )(sink, q, k, v, tab)


def _mix_out_kernel(x_ref, m_ref, g_ref, oa_ref, ob_ref, oc_ref, wg_ref, wa_ref, wb_ref, wc_ref,
                    wo_ref, o_ref):
    x = x_ref[0]
    d = x.shape[-1]
    h = _norm_mod(x, g_ref[...], m_ref[0, 0], m_ref[0, 1]).astype(BF16)
    gates = jnp.dot(h, wg_ref[...], preferred_element_type=F32)
    ya = jnp.dot(oa_ref[0], wa_ref[...], preferred_element_type=F32)
    yb = jnp.dot(ob_ref[0], wb_ref[...], preferred_element_type=F32)
    yc = jnp.dot(oc_ref[0], wc_ref[...], preferred_element_type=F32)
    merged = (_sigmoid(gates[:, :d]) * ya + _sigmoid(gates[:, d:2 * d]) * yb
              + _sigmoid(gates[:, 2 * d:]) * yc)
    out = jnp.dot(merged.astype(BF16), wo_ref[...], preferred_element_type=F32)
    o_ref[0] = x + m_ref[0, 2] * out


def _mix_out(x, mods, mod_idx, gain, o_a, o_b, o_c, w_gates, w_a, w_b, w_c, w_o):
    b, s, d = x.shape
    row = lambda w: pl.BlockSpec((1, ROW_TILE, w), lambda i, j: (i, j, 0))
    return pl.pallas_call(
        _mix_out_kernel,
        grid=(b, s // ROW_TILE),
        in_specs=[
            row(d),
            pl.BlockSpec((1, 3, 1, d), lambda i, j: (i, mod_idx, 0, 0)),
            _const_spec((1, d)),
            row(o_a.shape[2]), row(o_b.shape[2]), row(o_c.shape[2]),
            _const_spec(w_gates.shape), _const_spec(w_a.shape), _const_spec(w_b.shape),
            _const_spec(w_c.shape), _const_spec(w_o.shape),
        ],
        out_specs=row(d),
        out_shape=jax.ShapeDtypeStruct((b, s, d), F32),
        compiler_params=_params("arbitrary", "arbitrary"),
        name="mix_out",
    )(x, mods, gain.reshape(1, d), o_a, o_b, o_c, w_gates, w_a, w_b, w_c, w_o)


def _rope_angles(pos, dim):
    inv = ROPE_THETA ** (-jnp.arange(0, dim, 2, dtype=jnp.float32) / dim)
    ang = pos.astype(jnp.float32)[:, None] * inv[None, :]
    return jnp.cos(ang), jnp.sin(ang)


def _rope_tables(s):
    rows = s // GRID_W
    t = jnp.arange(s)
    row_pos = jnp.repeat(jnp.arange(rows), GRID_W)
    col_pos = jnp.tile(jnp.arange(GRID_W), rows)
    cr, sr = _rope_angles(row_pos, HEAD_DIM // 2)
    cc, sc = _rope_angles(col_pos, HEAD_DIM // 2)
    cos_h = jnp.concatenate([cr, cr, cc, cc], axis=-1)
    sin_h = jnp.concatenate([-sr, sr, -sc, sc], axis=-1)
    reps = LANES // HEAD_DIM
    cos_a = jnp.tile(cos_h, (1, reps))
    sin_a = jnp.tile(sin_h, (1, reps))
    cs, ss = _rope_angles(t, C_ROPE_DIM)
    ones = jnp.ones((s, C_NOPE_DIM), F32)
    zeros_lo = jnp.zeros((s, C_NOPE_DIM), F32)
    zeros_hi = jnp.zeros((s, LANES - C_QK_DIM), F32)
    t1 = jnp.concatenate([ones, cs, cs, zeros_hi], axis=-1)
    t2 = jnp.concatenate([zeros_lo, -ss, ss, zeros_hi], axis=-1)
    return cos_a, sin_a, t1, t2


def _attn_plans():
    plan_a = tuple((tuple((j, g, j, g) for j in range(Q_SLOTS)), 0, 0) for g in range(A_KV_HEADS))
    plan_c = tuple((((h, None, h // 2, h % 2),), h, h // 2) for h in range(C_HEADS))
    return plan_a, plan_c


def kernel(x, c, ada_w, ada_b, norm_ffn1, ffn1_w_gu, ffn1_w_down, norm_mix, w_in, a_q_norm, a_k_norm,
           b_sink, rel_bias, c_q_lat_norm, c_w_q_up, c_kv_lat_norm, c_w_kv_up, w_br_a, w_br_b, w_br_c,
           w_out, norm_ffn2, ffn2_w_gu, ffn2_w_down, final_norm):
    b, s, d = x.shape
    assert d == D_MODEL and s % max(Q_TILE, C_Q_TILE, ROW_TILE) == 0 and s >= WIN_KEYS and s % GRID_W == 0
    depth = ada_w.shape[0]

    mods = _ada_mods(c, ada_w, ada_b)
    tabs = _rope_tables(s)
    bias_tab = _bias_table(rel_bias)
    plan_a, plan_c = _attn_plans()

    proj_cols = _proj_columns()
    q_up_cols = _q_up_columns()
    kv_up_cols = _kv_up_columns()
    head_rows = np.asarray([h * HEAD_DIM + dd for h in _q_head_order() for dd in range(HEAD_DIM)])
    partner = np.arange(HEAD_DIM) ^ (HEAD_DIM // 4)
    reps = LANES // HEAD_DIM

    for l in range(depth):
        x = _ffn(x, mods, 3 * l, norm_ffn1[l], ffn1_w_gu[l], ffn1_w_down[l])

        w_attn = _take_cols(w_in[l], proj_cols).astype(BF16)
        aqn = jnp.stack([jnp.tile(a_q_norm[l], reps), jnp.tile(a_q_norm[l][partner], reps)])
        akn = jnp.stack([jnp.tile(a_k_norm[l], reps), jnp.tile(a_k_norm[l][partner], reps)])
        wq = _take_cols(c_w_q_up[l], q_up_cols).astype(BF16)
        wkv = _take_cols(c_w_kv_up[l], kv_up_cols).astype(BF16)
        qa, ka, va, qb, kb, vb, qc, kc, vc = _proj(
            x, mods, 3 * l + 1, norm_mix[l], w_attn, aqn, akn,
            c_q_lat_norm[l].reshape(1, -1), wq, c_kv_lat_norm[l].reshape(1, -1), wkv, tabs)

        o_a = _dense_attn(qa, ka, va, plan_a, Q_TILE, "attn_a")
        o_b = _win_attn(qb, kb, vb, bias_tab, b_sink[l])
        o_c = _dense_attn(qc, kc, vc, plan_c, C_Q_TILE, "attn_c")

        x = _mix_out(
            x, mods, 3 * l + 1, norm_mix[l], o_a, o_b, o_c,
            w_in[l][:, GATE_OFF:].astype(BF16),
            w_br_a[l][head_rows].astype(BF16), w_br_b[l][head_rows].astype(BF16),
            w_br_c[l].astype(BF16), w_out[l].astype(BF16))

        x = _ffn(x, mods, 3 * l + 2, norm_ffn2[l], ffn2_w_gu[l], ffn2_w_down[l],
                 final_gain=final_norm if l == depth - 1 else None)
    return x
```

```python
import functools
import math

import numpy as np
import jax
import jax.numpy as jnp
from jax import lax
from jax.experimental import pallas as pl
from jax.experimental.pallas import tpu as pltpu

D_MODEL = 1024
DEPTH = 2
HEAD_DIM = 64
A_HEADS = 6
A_KV_HEADS = 2
B_HEADS = 6
B_KV_HEADS = 2
C_HEADS = 4
C_Q_RANK = 256
C_KV_RANK = 128
C_NOPE_DIM = 64
C_ROPE_DIM = 32
C_V_DIM = 64
C_QK_DIM = C_NOPE_DIM + C_ROPE_DIM
D_FF = 2816
GRID_W = 64
WINDOW = 128
NUM_BUCKETS = 32
MAX_DISTANCE = 128
ROPE_THETA = 10000.0
ADA_CHUNKS = 9
EPS = 1e-6
NEG_INF = -1e30

A_Q_W = A_HEADS * HEAD_DIM
A_KV_W = A_KV_HEADS * HEAD_DIM
B_Q_W = B_HEADS * HEAD_DIM
B_KV_W = B_KV_HEADS * HEAD_DIM
IN_SIZES = (A_Q_W, A_KV_W, A_KV_W, B_Q_W, B_KV_W, B_KV_W,
            C_Q_RANK, C_KV_RANK, C_ROPE_DIM, D_MODEL, D_MODEL, D_MODEL)
IN_OFFS = tuple(int(v) for v in np.cumsum((0,) + IN_SIZES))
GATE_OFF = IN_OFFS[9]
ROPE_HALF = HEAD_DIM // 4
assert ROPE_HALF == C_ROPE_DIM // 2

LANES = 128
MXU_TILE = 256
V7X_VMEM_LIMIT_BYTES = 60000 * 1024

V_SLOT_W = 2 * LANES
LOG2_E = math.log2(math.e)
ROW_TILE = 1024
FFN_CHUNK = MXU_TILE
Q_TILE = 256
C_Q_TILE = 512
KEY_PIECE = 512
WIN_KEYS = Q_TILE + 2 * WINDOW
WIN_PLACEMENTS = 3

BF16 = jnp.bfloat16
F32 = jnp.float32


def _params(*sem):
    return pltpu.CompilerParams(dimension_semantics=sem, vmem_limit_bytes=V7X_VMEM_LIMIT_BYTES)


def _const_spec(shape):
    nd = len(shape)
    return pl.BlockSpec(shape, lambda *_: (0,) * nd, pipeline_mode=pl.Buffered(1))


def _sigmoid(v):
    return 1.0 / (1.0 + jnp.exp(-v))


def _norm_mod(x, gain, shift, scale):
    ms = jnp.mean(x * x, axis=-1, keepdims=True)
    y = x * lax.rsqrt(ms + EPS) * gain
    return y * (1.0 + scale) + shift


def _ada_kernel(c_ref, w_ref, b_ref, o_ref):
    c = c_ref[...]
    cond = (c * _sigmoid(c)).astype(BF16)
    o_ref[0] = jnp.dot(cond, w_ref[0].astype(BF16), preferred_element_type=F32) + b_ref[0]


def _ada_mods(c, ada_w, ada_b):
    depth, d, _ = ada_w.shape
    b = c.shape[0]
    bias = ada_b.reshape(depth * ADA_CHUNKS, 1, d)
    out = pl.pallas_call(
        _ada_kernel,
        grid=(depth, ADA_CHUNKS),
        in_specs=[
            pl.BlockSpec((b, d), lambda l, j: (0, 0)),
            pl.BlockSpec((1, d, d), lambda l, j: (l, 0, j)),
            pl.BlockSpec((1, 1, d), lambda l, j: (l * ADA_CHUNKS + j, 0, 0)),
        ],
        out_specs=pl.BlockSpec((1, b, d), lambda l, j: (l * ADA_CHUNKS + j, 0, 0)),
        out_shape=jax.ShapeDtypeStruct((depth * ADA_CHUNKS, b, d), F32),
        compiler_params=_params("arbitrary", "arbitrary"),
        name="ada_mods",
    )(c, ada_w, bias)
    return jnp.transpose(out, (1, 0, 2))[:, :, None, :]


def _ffn_kernel(x_ref, m_ref, g_ref, wgu_ref, wd_ref, *rest, final):
    o_ref = rest[-1]
    x = x_ref[0]
    h = _norm_mod(x, g_ref[...], m_ref[0, 0], m_ref[0, 1]).astype(BF16)
    y = None
    for c in range(D_FF // FFN_CHUNK):
        gu = jnp.dot(h, wgu_ref[:, c * 2 * FFN_CHUNK:(c + 1) * 2 * FFN_CHUNK], preferred_element_type=F32)
        gate = gu[:, :FFN_CHUNK]
        up = gu[:, FFN_CHUNK:]
        act = (gate * _sigmoid(gate) * up).astype(BF16)
        yc = jnp.dot(act, wd_ref[c * FFN_CHUNK:(c + 1) * FFN_CHUNK, :], preferred_element_type=F32)
        y = yc if c == 0 else y + yc
    out = x + (0.5 * m_ref[0, 2]) * y
    if final:
        ms = jnp.mean(out * out, axis=-1, keepdims=True)
        out = out * lax.rsqrt(ms + EPS) * rest[0][...]
    o_ref[0] = out


def _ffn_gu_columns():
    chunk = np.arange(FFN_CHUNK)
    return np.concatenate([np.concatenate([c * FFN_CHUNK + chunk, D_FF + c * FFN_CHUNK + chunk])
                           for c in range(D_FF // FFN_CHUNK)])


def _ffn(x, mods, mod_idx, gain, w_gu, w_down, final_gain=None):
    b, s, d = x.shape
    final = final_gain is not None
    in_specs = [
        pl.BlockSpec((1, ROW_TILE, d), lambda i, j: (i, j, 0)),
        pl.BlockSpec((1, 3, 1, d), lambda i, j: (i, mod_idx, 0, 0)),
        _const_spec((1, d)),
        _const_spec((d, 2 * D_FF)),
        _const_spec((D_FF, d)),
    ]
    args = [x, mods, gain.reshape(1, d), w_gu[:, _ffn_gu_columns()].astype(BF16), w_down.astype(BF16)]
    if final:
        in_specs.append(_const_spec((1, d)))
        args.append(final_gain.reshape(1, d))
    return pl.pallas_call(
        functools.partial(_ffn_kernel, final=final),
        grid=(b, s // ROW_TILE),
        in_specs=in_specs,
        out_specs=pl.BlockSpec((1, ROW_TILE, d), lambda i, j: (i, j, 0)),
        out_shape=jax.ShapeDtypeStruct((b, s, d), F32),
        compiler_params=_params("arbitrary", "arbitrary"),
        name="ffn_final" if final else "ffn",
    )(*args)


_SEG_SIZES = (A_Q_W, A_KV_W, A_KV_W, B_Q_W, B_KV_W, B_KV_W, C_Q_RANK, C_KV_RANK, LANES)
_SEG_OFFS = tuple(int(v) for v in np.cumsum((0,) + _SEG_SIZES))
PROJ_W = _SEG_OFFS[-1]
Q_SLOTS = A_HEADS // A_KV_HEADS
C_Q_W = C_HEADS * LANES
C_V_W = C_HEADS * C_V_DIM


def _q_head_order():
    g = A_HEADS // A_KV_HEADS
    return [k * g + j for j in range(g) for k in range(A_KV_HEADS)]


def _proj_columns():
    aq, ak, av, bq, bk, bv, cq, ckv, ckr = IN_OFFS[:9]
    order = _q_head_order()
    cols = []
    cols += [aq + h * HEAD_DIM + d for h in order for d in range(HEAD_DIM)]
    cols += list(range(ak, ak + A_KV_W))
    cols += list(range(av, av + A_KV_W))
    cols += [bq + h * HEAD_DIM + d for h in order for d in range(HEAD_DIM)]
    cols += list(range(bk, bk + B_KV_W))
    cols += list(range(bv, bv + B_KV_W))
    cols += list(range(cq, cq + C_Q_RANK))
    cols += list(range(ckv, ckv + C_KV_RANK))
    cols += [-1] * C_NOPE_DIM + [ckr + d for d in range(C_ROPE_DIM)] + [-1] * (LANES - C_QK_DIM)
    cols = np.asarray(cols, np.int32)
    assert cols.shape[0] == PROJ_W
    return cols


def _take_cols(w, cols):
    picked = jnp.take(w, jnp.asarray(np.maximum(cols, 0)), axis=1)
    return jnp.where(jnp.asarray(cols >= 0)[None, :], picked, 0.0)


def _q_up_columns():
    rdims = np.arange(C_ROPE_DIM)
    rpartner = rdims ^ ROPE_HALF
    pad_hi = LANES - C_QK_DIM
    direct, part = [], []
    for h in range(C_HEADS):
        base = h * C_QK_DIM
        direct += [base + d for d in range(C_NOPE_DIM)] + [base + C_NOPE_DIM + d for d in rdims] + [-1] * pad_hi
        part += [-1] * C_NOPE_DIM + [base + C_NOPE_DIM + d for d in rpartner] + [-1] * pad_hi
    return np.asarray(direct + part, np.int32)


def _kv_up_columns():
    per = C_NOPE_DIM + C_V_DIM
    keys, vals = [], []
    for h in range(C_HEADS):
        keys += [h * per + d for d in range(C_NOPE_DIM)] + [-1] * (LANES - C_NOPE_DIM)
        vals += [h * per + C_NOPE_DIM + d for d in range(C_V_DIM)]
    return np.asarray(keys + vals, np.int32)


def _rope_partner(t):
    lane = lax.broadcasted_iota(jnp.int32, t.shape, 1)
    ahead = pltpu.roll(t, LANES - ROPE_HALF, 1)
    behind = pltpu.roll(t, ROPE_HALF, 1)
    return jnp.where((lane & ROPE_HALF) == 0, ahead, behind)


def _half_rms(t, n):
    lane = lax.broadcasted_iota(jnp.int32, t.shape, 1)
    left = lane < HEAD_DIM
    sq = t * t
    s_l = jnp.sum(jnp.where(left, sq, 0.0), axis=-1, keepdims=True)
    s_r = jnp.sum(jnp.where(left, 0.0, sq), axis=-1, keepdims=True)
    return jnp.where(left, lax.rsqrt(s_l / n + EPS), lax.rsqrt(s_r / n + EPS))


def _proj_kernel(x_ref, m_ref, g_ref, w_ref, aqn_ref, akn_ref, cqn_ref, wq_ref, ckvn_ref, wkv_ref,
                 ac_ref, as_ref, t1_ref, t2_ref,
                 qa_ref, ka_ref, va_ref, qb_ref, kb_ref, vb_ref, qc_ref, kc_ref, vc_ref):
    x = x_ref[0]
    h = _norm_mod(x, g_ref[...], m_ref[0, 0], m_ref[0, 1]).astype(BF16)
    p = jnp.dot(h, w_ref[...], preferred_element_type=F32)
    seg = [p[:, _SEG_OFFS[i]:_SEG_OFFS[i + 1]] for i in range(len(_SEG_SIZES))]
    aq, ak, av, bq, bk, bv, cq, ckv, kr = seg

    cos_a = ac_ref[...]
    sin_a = as_ref[...]
    qa_scale = HEAD_DIM ** -0.5 * LOG2_E
    qb_scale = HEAD_DIM ** -0.5 * LOG2_E
    qc_scale = C_QK_DIM ** -0.5 * LOG2_E
    ones = jnp.ones((x.shape[0], LANES), BF16)
    gq_c = aqn_ref[0:1, :] * cos_a
    gq_s = aqn_ref[1:2, :] * sin_a
    for j in range(Q_SLOTS):
        sl = slice(j * LANES, (j + 1) * LANES)
        r = _half_rms(aq[:, sl], float(HEAD_DIM))
        qa_ref[0, :, sl] = ((aq[:, sl] * gq_c + _rope_partner(aq[:, sl]) * gq_s) * (r * qa_scale)).astype(BF16)
        qb_ref[0, :, sl] = (bq[:, sl] * qb_scale).astype(BF16)
    r = _half_rms(ak, float(HEAD_DIM))
    ka_ref[0] = ((ak * (akn_ref[0:1, :] * cos_a) + _rope_partner(ak) * (akn_ref[1:2, :] * sin_a)) * r).astype(BF16)
    va_ref[0, :, :LANES] = av.astype(BF16)
    va_ref[0, :, LANES:] = ones
    kb_ref[0] = bk.astype(BF16)
    vb_ref[0, :, :LANES] = bv.astype(BF16)
    vb_ref[0, :, LANES:] = ones

    t1 = t1_ref[...]
    t2 = t2_ref[...]
    cqn = cq * lax.rsqrt(jnp.mean(cq * cq, axis=-1, keepdims=True) + EPS) * cqn_ref[...]
    qu = jnp.dot(cqn.astype(BF16), wq_ref[...], preferred_element_type=F32)
    ckvn = ckv * lax.rsqrt(jnp.mean(ckv * ckv, axis=-1, keepdims=True) + EPS) * ckvn_ref[...]
    kvu = jnp.dot(ckvn.astype(BF16), wkv_ref[...], preferred_element_type=F32)
    k_rope = kr * t1 + _rope_partner(kr) * t2
    for hh in range(C_HEADS):
        sl = slice(hh * LANES, (hh + 1) * LANES)
        sl2 = slice(C_Q_W + hh * LANES, C_Q_W + (hh + 1) * LANES)
        qc_ref[0, :, sl] = ((qu[:, sl] * t1 + qu[:, sl2] * t2) * qc_scale).astype(BF16)
        kc_ref[0, :, sl] = (kvu[:, sl] + k_rope).astype(BF16)
    for j in range(C_HEADS // 2):
        vc_ref[0, :, j * V_SLOT_W:j * V_SLOT_W + LANES] = kvu[:, C_Q_W + j * LANES:C_Q_W + (j + 1) * LANES].astype(BF16)
        vc_ref[0, :, j * V_SLOT_W + LANES:(j + 1) * V_SLOT_W] = ones


def _proj(x, mods, mod_idx, gain, w_attn, aqn, akn, cqn, wq, ckvn, wkv, tabs):
    b, s, d = x.shape
    row = lambda w: pl.BlockSpec((1, ROW_TILE, w), lambda i, j: (i, j, 0))
    tab = pl.BlockSpec((ROW_TILE, LANES), lambda i, j: (j, 0))
    widths = (A_Q_W, A_KV_W, V_SLOT_W, B_Q_W, B_KV_W, V_SLOT_W, C_Q_W, C_Q_W, C_HEADS // 2 * V_SLOT_W)
    return pl.pallas_call(
        _proj_kernel,
        grid=(b, s // ROW_TILE),
        in_specs=[
            row(d),
            pl.BlockSpec((1, 3, 1, d), lambda i, j: (i, mod_idx, 0, 0)),
            _const_spec((1, d)),
            _const_spec((d, PROJ_W)),
            _const_spec((2, LANES)),
            _const_spec((2, LANES)),
            _const_spec((1, C_Q_RANK)),
            _const_spec((C_Q_RANK, 2 * C_Q_W)),
            _const_spec((1, C_KV_RANK)),
            _const_spec((C_KV_RANK, C_Q_W + C_V_W)),
            tab, tab, tab, tab,
        ],
        out_specs=[row(w) for w in widths],
        out_shape=[jax.ShapeDtypeStruct((b, s, w), BF16) for w in widths],
        compiler_params=_params("arbitrary", "arbitrary"),
        name="attn_proj",
    )(x, mods, gain.reshape(1, d), w_attn, aqn, akn, cqn, wq, ckvn, wkv, *tabs)


def _lane_left(shape):
    return lax.broadcasted_iota(jnp.int32, shape, 1) < HEAD_DIM


def _dense_attn_steps(q_ref, k_ref, v_ref, o_ref, s_new, m_new, s_old, m_old, *, plan):
    tq = q_ref.shape[1]
    n_keys = k_ref.shape[1]
    left = _lane_left((tq, LANES))
    row = 0
    pieces = {}
    for heads, k_slot, v_slot in plan:
        qs = []
        for q_slot, q_half, _, _ in heads:
            q = q_ref[0, :, q_slot * LANES:(q_slot + 1) * LANES]
            if q_half is not None:
                q = jnp.where(left if q_half == 0 else jnp.logical_not(left), q, jnp.zeros_like(q))
            qs.append(q)
        q = qs[0] if len(qs) == 1 else jnp.concatenate(qs, axis=0)
        rows = slice(row, row + q.shape[0])
        row += q.shape[0]
        m_prev = m_old[rows, :]
        m_run = o_run = None
        for c in range(n_keys // KEY_PIECE):
            cols = slice(c * KEY_PIECE, (c + 1) * KEY_PIECE)
            k = k_ref[0, cols, k_slot * LANES:(k_slot + 1) * LANES]
            s = lax.dot_general(q, k, (((1,), (1,)), ((), ())), preferred_element_type=F32)
            s_new[rows, cols] = s
            mx = jnp.max(s, axis=-1, keepdims=True)
            m_run = mx if c == 0 else jnp.maximum(m_run, mx)

            p = jnp.exp2(s_old[rows, cols] - m_prev)
            v = v_ref[0, cols, v_slot * V_SLOT_W:(v_slot + 1) * V_SLOT_W]
            pv = jnp.dot(p.astype(BF16), v, preferred_element_type=F32)
            o_run = pv if c == 0 else o_run + pv
        m_new[rows, :] = m_run
        o = o_run[:, :LANES] * (1.0 / o_run[:, LANES:])
        for n, (_, _, out_slot, out_half) in enumerate(heads):
            pieces[(out_slot, out_half)] = o[n * tq:(n + 1) * tq]
    for j in range(o_ref.shape[2] // LANES):
        o_ref[0, :, j * LANES:(j + 1) * LANES] = jnp.where(left, pieces[(j, 0)], pieces[(j, 1)]).astype(BF16)


def _win_start(tile, nq, s_len):
    q0 = (tile % nq) * Q_TILE
    start = jnp.clip(q0 - WINDOW, 0, s_len - WIN_KEYS)
    place = (start - q0 + 2 * WINDOW) // WINDOW
    return pl.multiple_of(start, LANES), place


def _win_attn_steps(sink_ref, q_ref, k_ref, v_ref, tab_ref, o_ref, s_new, m_new, s_old, m_old, *,
                    nq, tiles):
    t = pl.program_id(0)
    s_len = k_ref.shape[1]
    start_new, place = _win_start(jnp.minimum(t, tiles - 1), nq, s_len)
    start_old, _ = _win_start(jnp.maximum(t - 1, 0), nq, s_len)
    kw = k_ref[0, pl.ds(start_new, WIN_KEYS), :]
    vw = v_ref[0, pl.ds(start_old, WIN_KEYS), :]
    left = _lane_left((Q_TILE, LANES))
    g = B_HEADS // B_KV_HEADS
    slots = [q_ref[0, :, j * LANES:(j + 1) * LANES] for j in range(Q_SLOTS)]
    halves = []
    for side in range(B_KV_HEADS):
        keep = left if side == 0 else jnp.logical_not(left)
        q = jnp.concatenate([jnp.where(keep, qs, jnp.zeros_like(qs)) for qs in slots], axis=0)
        s = lax.dot_general(q, kw, (((1,), (1,)), ((), ())), preferred_element_type=F32)
        rows = slice(side * g * Q_TILE, (side + 1) * g * Q_TILE)
        sinks = [sink_ref[side * g + j] * LOG2_E for j in range(g)]
        sb, ms = [], []
        for j in range(g):
            sj = s[j * Q_TILE:(j + 1) * Q_TILE] + tab_ref[place, side * g + j]
            sb.append(sj)
            ms.append(jnp.maximum(jnp.max(sj, axis=-1, keepdims=True), sinks[j]))
        s_new[rows, :] = jnp.concatenate(sb, axis=0)
        m_new[rows, :] = jnp.concatenate(ms, axis=0)

        m_prev = m_old[rows, :]
        p = jnp.exp2(s_old[rows, :] - m_prev)
        pv = jnp.dot(p.astype(BF16), vw, preferred_element_type=F32)
        outs = []
        for j in range(g):
            hr = slice(j * Q_TILE, (j + 1) * Q_TILE)
            l = pv[hr, LANES:] + jnp.exp2(sinks[j] - m_prev[hr])
            outs.append(pv[hr, :LANES] * (1.0 / l))
        halves.append(outs)
    for j in range(Q_SLOTS):
        o_ref[0, :, j * LANES:(j + 1) * LANES] = jnp.where(left, halves[0][j], halves[1][j]).astype(BF16)


def _zero_init(t, refs):
    @pl.when(t == 0)
    def _():
        for ref in refs:
            ref[...] = jnp.zeros(ref.shape, F32)


def _dense_attn_kernel(q_ref, k_ref, v_ref, o_ref, s0, m0, s1, m1, *, plan):
    t = pl.program_id(0)
    step = functools.partial(_dense_attn_steps, q_ref, k_ref, v_ref, o_ref, plan=plan)
    _zero_init(t, (s1, m1))

    @pl.when(t % 2 == 0)
    def _():
        step(s0, m0, s1, m1)

    @pl.when(t % 2 == 1)
    def _():
        step(s1, m1, s0, m0)


def _attn_ab_kernel(sink_ref, qa_ref, ka_ref, va_ref, qb_ref, kb_ref, vb_ref, tab_ref, oa_ref, ob_ref,
                    sa0, ma0, sa1, ma1, sb0, mb0, sb1, mb1, *, plan, nq, tiles):
    t = pl.program_id(0)

    def step(sa_new, ma_new, sa_old, ma_old, sb_new, mb_new, sb_old, mb_old):
        _win_attn_steps(sink_ref, qb_ref, kb_ref, vb_ref, tab_ref, ob_ref, sb_new, mb_new, sb_old, mb_old,
                        nq=nq, tiles=tiles)
        _dense_attn_steps(qa_ref, ka_ref, va_ref, oa_ref, sa_new, ma_new, sa_old, ma_old, plan=plan)

    _zero_init(t, (sa1, ma1, sb1, mb1))

    @pl.when(t % 2 == 0)
    def _():
        step(sa0, ma0, sa1, ma1, sb0, mb0, sb1, mb1)

    @pl.when(t % 2 == 1)
    def _():
        step(sa1, ma1, sa0, ma0, sb1, mb1, sb0, mb0)


def _pipeline_specs(b, s, q_tile):
    nq = s // q_tile
    tiles = b * nq

    def cur(t):
        return jnp.minimum(t, tiles - 1)

    def prev(t):
        return jnp.maximum(t - 1, 0)

    q_spec = lambda w: pl.BlockSpec((1, q_tile, w), lambda t: (cur(t) // nq, cur(t) % nq, 0))
    k_spec = lambda w: pl.BlockSpec((1, s, w), lambda t: (cur(t) // nq, 0, 0))
    v_spec = lambda w: pl.BlockSpec((1, s, w), lambda t: (prev(t) // nq, 0, 0))
    o_spec = lambda w: pl.BlockSpec((1, q_tile, w), lambda t: (prev(t) // nq, prev(t) % nq, 0))
    return nq, tiles, q_spec, k_spec, v_spec, o_spec


def _score_scratch(rows, keys):
    return [pltpu.VMEM(shape, F32) for _ in range(2) for shape in ((rows, keys), (rows, 1))]


def _dense_attn(q, k, v, plan, q_tile, name):
    b, s, qw = q.shape
    ow = (max(h[2] for g in plan for h in g[0]) + 1) * LANES
    stacked_rows = sum(len(g[0]) for g in plan) * q_tile
    _, tiles, q_spec, k_spec, v_spec, o_spec = _pipeline_specs(b, s, q_tile)
    return pl.pallas_call(
        functools.partial(_dense_attn_kernel, plan=plan),
        grid=(tiles + 1,),
        in_specs=[q_spec(qw), k_spec(k.shape[2]), v_spec(v.shape[2])],
        out_specs=o_spec(ow),
        out_shape=jax.ShapeDtypeStruct((b, s, ow), BF16),
        scratch_shapes=_score_scratch(stacked_rows, s),
        compiler_params=_params("arbitrary"),
        name=name,
    )(q, k, v)


def _attn_ab(qa, ka, va, qb, kb, vb, tab, sink, plan):
    b, s, qw = qa.shape
    rows_a = sum(len(g[0]) for g in plan) * Q_TILE
    nq, tiles, q_spec, k_spec, v_spec, o_spec = _pipeline_specs(b, s, Q_TILE)
    return pl.pallas_call(
        functools.partial(_attn_ab_kernel, plan=plan, nq=nq, tiles=tiles),
        grid=(tiles + 1,),
        in_specs=[
            pl.BlockSpec(memory_space=pltpu.SMEM),
            q_spec(qw), k_spec(ka.shape[2]), v_spec(va.shape[2]),
            q_spec(qb.shape[2]), k_spec(kb.shape[2]), v_spec(vb.shape[2]),
            _const_spec(tab.shape),
        ],
        out_specs=[o_spec(qw), o_spec(qb.shape[2])],
        out_shape=[jax.ShapeDtypeStruct((b, s, qw), BF16), jax.ShapeDtypeStruct((b, s, qb.shape[2]), BF16)],
        scratch_shapes=_score_scratch(rows_a, s) + _score_scratch(B_HEADS * Q_TILE, WIN_KEYS),
        compiler_params=_params("arbitrary"),
        name="attn_ab",
    )(sink, qa, ka, va, qb, kb, vb, tab)


def _bias_tab_kernel(rb_ref, idx_ref, o_ref):
    for p in range(WIN_PLACEMENTS):
        idx = idx_ref[p]
        for h in range(B_HEADS):
            acc = jnp.full(idx.shape, NEG_INF, F32)
            for bkt in range(NUM_BUCKETS):
                acc = jnp.where(idx == bkt, rb_ref[bkt, h] * LOG2_E, acc)
            o_ref[p, h] = acc


def _t5_bucket(rel):
    nb = NUM_BUCKETS // 2
    max_exact = nb // 2
    ret = jnp.where(rel > 0, nb, 0)
    n = jnp.abs(rel)
    large = max_exact + (jnp.log(jnp.maximum(n, 1).astype(jnp.float32) / max_exact)
                         / math.log(MAX_DISTANCE / max_exact) * (nb - max_exact)).astype(jnp.int32)
    large = jnp.minimum(large, nb - 1)
    return ret + jnp.where(n < max_exact, n, large)


def _bias_table(rel_bias):
    p = jnp.arange(WIN_PLACEMENTS)[:, None, None]
    r = jnp.arange(Q_TILE)[None, :, None]
    j = jnp.arange(WIN_KEYS)[None, None, :]
    rel = j + p * WINDOW - 2 * WINDOW - r
    idx = jnp.where(jnp.abs(rel) <= WINDOW, _t5_bucket(rel), -1).astype(jnp.int32)
    shape = (WIN_PLACEMENTS, B_HEADS, Q_TILE, WIN_KEYS)
    return pl.pallas_call(
        _bias_tab_kernel,
        in_specs=[pl.BlockSpec(memory_space=pltpu.SMEM),
                  pl.BlockSpec(idx.shape, lambda: (0, 0, 0))],
        out_specs=pl.BlockSpec(shape, lambda: (0, 0, 0, 0)),
        out_shape=jax.ShapeDtypeStruct(shape, F32),
        compiler_params=pltpu.CompilerParams(vmem_limit_bytes=V7X_VMEM_LIMIT_BYTES),
        name="win_bias_table",
    )(rel_bias, idx)


def _mix_out_kernel(x_ref, m_ref, g_ref, oa_ref, ob_ref, oc_ref, wg_ref, wa_ref, wb_ref, wc_ref,
                    wo_ref, o_ref):
    x = x_ref[0]
    d = x.shape[-1]
    h = _norm_mod(x, g_ref[...], m_ref[0, 0], m_ref[0, 1]).astype(BF16)
    gates = jnp.dot(h, wg_ref[...], preferred_element_type=F32)
    ya = jnp.dot(oa_ref[0], wa_ref[...], preferred_element_type=F32)
    yb = jnp.dot(ob_ref[0], wb_ref[...], preferred_element_type=F32)
    yc = jnp.dot(oc_ref[0], wc_ref[...], preferred_element_type=F32)
    merged = (_sigmoid(gates[:, :d]) * ya + _sigmoid(gates[:, d:2 * d]) * yb
              + _sigmoid(gates[:, 2 * d:]) * yc)
    out = jnp.dot(merged.astype(BF16), wo_ref[...], preferred_element_type=F32)
    o_ref[0] = x + m_ref[0, 2] * out


def _mix_out(x, mods, mod_idx, gain, o_a, o_b, o_c, w_gates, w_a, w_b, w_c, w_o):
    b, s, d = x.shape
    row = lambda w: pl.BlockSpec((1, ROW_TILE, w), lambda i, j: (i, j, 0))
    return pl.pallas_call(
        _mix_out_kernel,
        grid=(b, s // ROW_TILE),
        in_specs=[
            row(d),
            pl.BlockSpec((1, 3, 1, d), lambda i, j: (i, mod_idx, 0, 0)),
            _const_spec((1, d)),
            row(o_a.shape[2]), row(o_b.shape[2]), row(o_c.shape[2]),
            _const_spec(w_gates.shape), _const_spec(w_a.shape), _const_spec(w_b.shape),
            _const_spec(w_c.shape), _const_spec(w_o.shape),
        ],
        out_specs=row(d),
        out_shape=jax.ShapeDtypeStruct((b, s, d), F32),
        compiler_params=_params("arbitrary", "arbitrary"),
        name="mix_out",
    )(x, mods, gain.reshape(1, d), o_a, o_b, o_c, w_gates, w_a, w_b, w_c, w_o)


def _rope_angles(pos, dim):
    inv = ROPE_THETA ** (-jnp.arange(0, dim, 2, dtype=jnp.float32) / dim)
    ang = pos.astype(jnp.float32)[:, None] * inv[None, :]
    return jnp.cos(ang), jnp.sin(ang)


def _rope_tables(s):
    rows = s // GRID_W
    t = jnp.arange(s)
    row_pos = jnp.repeat(jnp.arange(rows), GRID_W)
    col_pos = jnp.tile(jnp.arange(GRID_W), rows)
    cr, sr = _rope_angles(row_pos, HEAD_DIM // 2)
    cc, sc = _rope_angles(col_pos, HEAD_DIM // 2)
    cos_h = jnp.concatenate([cr, cr, cc, cc], axis=-1)
    sin_h = jnp.concatenate([-sr, sr, -sc, sc], axis=-1)
    reps = LANES // HEAD_DIM
    cos_a = jnp.tile(cos_h, (1, reps))
    sin_a = jnp.tile(sin_h, (1, reps))
    cs, ss = _rope_angles(t, C_ROPE_DIM)
    ones = jnp.ones((s, C_NOPE_DIM), F32)
    zeros_lo = jnp.zeros((s, C_NOPE_DIM), F32)
    zeros_hi = jnp.zeros((s, LANES - C_QK_DIM), F32)
    t1 = jnp.concatenate([ones, cs, cs, zeros_hi], axis=-1)
    t2 = jnp.concatenate([zeros_lo, -ss, ss, zeros_hi], axis=-1)
    return cos_a, sin_a, t1, t2


def _attn_plans():
    plan_a = tuple((tuple((j, g, j, g) for j in range(Q_SLOTS)), 0, 0) for g in range(A_KV_HEADS))
    plan_c = tuple((((h, None, h // 2, h % 2),), h, h // 2) for h in range(C_HEADS))
    return plan_a, plan_c


def kernel(x, c, ada_w, ada_b, norm_ffn1, ffn1_w_gu, ffn1_w_down, norm_mix, w_in, a_q_norm, a_k_norm,
           b_sink, rel_bias, c_q_lat_norm, c_w_q_up, c_kv_lat_norm, c_w_kv_up, w_br_a, w_br_b, w_br_c,
           w_out, norm_ffn2, ffn2_w_gu, ffn2_w_down, final_norm):
    b, s, d = x.shape
    assert d == D_MODEL and s % max(Q_TILE, C_Q_TILE, ROW_TILE) == 0 and s >= WIN_KEYS and s % GRID_W == 0
    depth = ada_w.shape[0]

    mods = _ada_mods(c, ada_w, ada_b)
    tabs = _rope_tables(s)
    bias_tab = _bias_table(rel_bias)
    plan_a, plan_c = _attn_plans()

    proj_cols = _proj_columns()
    q_up_cols = _q_up_columns()
    kv_up_cols = _kv_up_columns()
    head_rows = np.asarray([h * HEAD_DIM + dd for h in _q_head_order() for dd in range(HEAD_DIM)])
    partner = np.arange(HEAD_DIM) ^ ROPE_HALF
    reps = LANES // HEAD_DIM

    for l in range(depth):
        x = _ffn(x, mods, 3 * l, norm_ffn1[l], ffn1_w_gu[l], ffn1_w_down[l])

        w_attn = _take_cols(w_in[l], proj_cols).astype(BF16)
        aqn = jnp.stack([jnp.tile(a_q_norm[l], reps), jnp.tile(a_q_norm[l][partner], reps)])
        akn = jnp.stack([jnp.tile(a_k_norm[l], reps), jnp.tile(a_k_norm[l][partner], reps)])
        wq = _take_cols(c_w_q_up[l], q_up_cols).astype(BF16)
        wkv = _take_cols(c_w_kv_up[l], kv_up_cols).astype(BF16)
        qa, ka, va, qb, kb, vb, qc, kc, vc = _proj(
            x, mods, 3 * l + 1, norm_mix[l], w_attn, aqn, akn,
            c_q_lat_norm[l].reshape(1, -1), wq, c_kv_lat_norm[l].reshape(1, -1), wkv, tabs)

        o_a, o_b = _attn_ab(qa, ka, va, qb, kb, vb, bias_tab, b_sink[l], plan_a)
        o_c = _dense_attn(qc, kc, vc, plan_c, C_Q_TILE, "attn_c")

        x = _mix_out(
            x, mods, 3 * l + 1, norm_mix[l], o_a, o_b, o_c,
            w_in[l][:, GATE_OFF:].astype(BF16),
            w_br_a[l][head_rows].astype(BF16), w_br_b[l][head_rows].astype(BF16),
            w_br_c[l].astype(BF16), w_out[l].astype(BF16))

        x = _ffn(x, mods, 3 * l + 2, norm_ffn2[l], ffn2_w_gu[l], ffn2_w_down[l],
                 final_gain=final_norm if l == depth - 1 else None)
    return x
```

```python
import functools
import math

import numpy as np
import jax
import jax.numpy as jnp
from jax import lax
from jax.experimental import pallas as pl
from jax.experimental.pallas import tpu as pltpu

D_MODEL = 1024
DEPTH = 2
HEAD_DIM = 64
A_HEADS = 6
A_KV_HEADS = 2
B_HEADS = 6
B_KV_HEADS = 2
C_HEADS = 4
C_Q_RANK = 256
C_KV_RANK = 128
C_NOPE_DIM = 64
C_ROPE_DIM = 32
C_V_DIM = 64
C_QK_DIM = C_NOPE_DIM + C_ROPE_DIM
D_FF = 2816
GRID_W = 64
WINDOW = 128
NUM_BUCKETS = 32
MAX_DISTANCE = 128
ROPE_THETA = 10000.0
ADA_CHUNKS = 9
EPS = 1e-6
NEG_INF = -1e30

A_Q_W = A_HEADS * HEAD_DIM
A_KV_W = A_KV_HEADS * HEAD_DIM
B_Q_W = B_HEADS * HEAD_DIM
B_KV_W = B_KV_HEADS * HEAD_DIM
IN_SIZES = (A_Q_W, A_KV_W, A_KV_W, B_Q_W, B_KV_W, B_KV_W,
            C_Q_RANK, C_KV_RANK, C_ROPE_DIM, D_MODEL, D_MODEL, D_MODEL)
IN_OFFS = tuple(int(v) for v in np.cumsum((0,) + IN_SIZES))
GATE_OFF = IN_OFFS[9]
ROPE_HALF = HEAD_DIM // 4
assert ROPE_HALF == C_ROPE_DIM // 2

LANES = 128
MXU_TILE = 256
V7X_VMEM_LIMIT_BYTES = 60000 * 1024

V_SLOT_W = 2 * LANES
VT_ONES = 16
VT_HEAD_H = HEAD_DIM + VT_ONES
VT_SLOT_H = 2 * VT_HEAD_H
LOG2_E = math.log2(math.e)
ROW_TILE = 1024
FFN_CHUNK = MXU_TILE
Q_TILE = 256
C_Q_TILE = 512
KEY_PIECE = 512
WIN_KEYS = Q_TILE + 2 * WINDOW
WIN_PLACEMENTS = 3

BF16 = jnp.bfloat16
F32 = jnp.float32


def _params(*sem):
    return pltpu.CompilerParams(dimension_semantics=sem, vmem_limit_bytes=V7X_VMEM_LIMIT_BYTES)


def _const_spec(shape):
    nd = len(shape)
    return pl.BlockSpec(shape, lambda *_: (0,) * nd, pipeline_mode=pl.Buffered(1))


def _sigmoid(v):
    return 1.0 / (1.0 + jnp.exp(-v))


def _norm_mod(x, gain, shift, scale):
    ms = jnp.mean(x * x, axis=-1, keepdims=True)
    y = x * lax.rsqrt(ms + EPS) * gain
    return y * (1.0 + scale) + shift


def _ada_kernel(c_ref, w_ref, b_ref, o_ref):
    c = c_ref[...]
    cond = (c * _sigmoid(c)).astype(BF16)
    o_ref[0] = jnp.dot(cond, w_ref[0].astype(BF16), preferred_element_type=F32) + b_ref[0]


def _ada_mods(c, ada_w, ada_b):
    depth, d, _ = ada_w.shape
    b = c.shape[0]
    bias = ada_b.reshape(depth * ADA_CHUNKS, 1, d)
    out = pl.pallas_call(
        _ada_kernel,
        grid=(depth, ADA_CHUNKS),
        in_specs=[
            pl.BlockSpec((b, d), lambda l, j: (0, 0)),
            pl.BlockSpec((1, d, d), lambda l, j: (l, 0, j)),
            pl.BlockSpec((1, 1, d), lambda l, j: (l * ADA_CHUNKS + j, 0, 0)),
        ],
        out_specs=pl.BlockSpec((1, b, d), lambda l, j: (l * ADA_CHUNKS + j, 0, 0)),
        out_shape=jax.ShapeDtypeStruct((depth * ADA_CHUNKS, b, d), F32),
        compiler_params=_params("arbitrary", "arbitrary"),
        name="ada_mods",
    )(c, ada_w, bias)
    return jnp.transpose(out, (1, 0, 2))[:, :, None, :]


def _ffn_kernel(x_ref, m_ref, g_ref, wgu_ref, wd_ref, *rest, final):
    o_ref = rest[-1]
    x = x_ref[0]
    h = _norm_mod(x, g_ref[...], m_ref[0, 0], m_ref[0, 1]).astype(BF16)
    y = None
    for c in range(D_FF // FFN_CHUNK):
        gu = jnp.dot(h, wgu_ref[:, c * 2 * FFN_CHUNK:(c + 1) * 2 * FFN_CHUNK], preferred_element_type=F32)
        gate = gu[:, :FFN_CHUNK]
        up = gu[:, FFN_CHUNK:]
        act = (gate * _sigmoid(gate) * up).astype(BF16)
        yc = jnp.dot(act, wd_ref[c * FFN_CHUNK:(c + 1) * FFN_CHUNK, :], preferred_element_type=F32)
        y = yc if c == 0 else y + yc
    out = x + (0.5 * m_ref[0, 2]) * y
    if final:
        ms = jnp.mean(out * out, axis=-1, keepdims=True)
        out = out * lax.rsqrt(ms + EPS) * rest[0][...]
    o_ref[0] = out


def _ffn(x, mods, mod_idx, gain, w_gu, w_down, final_gain=None):
    b, s, d = x.shape
    final = final_gain is not None
    in_specs = [
        pl.BlockSpec((1, ROW_TILE, d), lambda i, j: (i, j, 0)),
        pl.BlockSpec((1, 3, 1, d), lambda i, j: (i, mod_idx, 0, 0)),
        _const_spec((1, d)),
        _const_spec((d, 2 * D_FF)),
        _const_spec((D_FF, d)),
    ]
    n_chunks = D_FF // FFN_CHUNK
    w_gu = w_gu.reshape(d, 2, n_chunks, FFN_CHUNK).transpose(0, 2, 1, 3).reshape(d, 2 * D_FF)
    args = [x, mods, gain.reshape(1, d), w_gu.astype(BF16), w_down.astype(BF16)]
    if final:
        in_specs.append(_const_spec((1, d)))
        args.append(final_gain.reshape(1, d))
    return pl.pallas_call(
        functools.partial(_ffn_kernel, final=final),
        grid=(b, s // ROW_TILE),
        in_specs=in_specs,
        out_specs=pl.BlockSpec((1, ROW_TILE, d), lambda i, j: (i, j, 0)),
        out_shape=jax.ShapeDtypeStruct((b, s, d), F32),
        compiler_params=_params("arbitrary", "arbitrary"),
        name="ffn_final" if final else "ffn",
    )(*args)


_SEG_SIZES = (A_Q_W, A_KV_W, A_KV_W, B_Q_W, B_KV_W, B_KV_W, C_Q_RANK, C_KV_RANK, LANES)
_SEG_OFFS = tuple(int(v) for v in np.cumsum((0,) + _SEG_SIZES))
PROJ_W = _SEG_OFFS[-1]
Q_SLOTS = A_HEADS // A_KV_HEADS
C_Q_W = C_HEADS * LANES
C_V_W = C_HEADS * C_V_DIM


def _q_head_order():
    g = A_HEADS // A_KV_HEADS
    return [k * g + j for j in range(g) for k in range(A_KV_HEADS)]


def _proj_columns():
    aq, ak, av, bq, bk, bv, cq, ckv, ckr = IN_OFFS[:9]
    order = _q_head_order()
    cols = []
    cols += [aq + h * HEAD_DIM + d for h in order for d in range(HEAD_DIM)]
    cols += list(range(ak, ak + A_KV_W))
    cols += list(range(av, av + A_KV_W))
    cols += [bq + h * HEAD_DIM + d for h in order for d in range(HEAD_DIM)]
    cols += list(range(bk, bk + B_KV_W))
    cols += list(range(bv, bv + B_KV_W))
    cols += list(range(cq, cq + C_Q_RANK))
    cols += list(range(ckv, ckv + C_KV_RANK))
    cols += [-1] * C_NOPE_DIM + [ckr + d for d in range(C_ROPE_DIM)] + [-1] * (LANES - C_QK_DIM)
    cols = np.asarray(cols, np.int32)
    assert cols.shape[0] == PROJ_W
    return cols


def _take_cols(w, cols):
    picked = jnp.take(w, jnp.asarray(np.maximum(cols, 0)), axis=1)
    return jnp.where(jnp.asarray(cols >= 0)[None, :], picked, 0.0)


def _q_up_columns():
    rdims = np.arange(C_ROPE_DIM)
    rpartner = rdims ^ ROPE_HALF
    pad_hi = LANES - C_QK_DIM
    direct, part = [], []
    for h in range(C_HEADS):
        base = h * C_QK_DIM
        direct += [base + d for d in range(C_NOPE_DIM)] + [base + C_NOPE_DIM + d for d in rdims] + [-1] * pad_hi
        part += [-1] * C_NOPE_DIM + [base + C_NOPE_DIM + d for d in rpartner] + [-1] * pad_hi
    return np.asarray(direct + part, np.int32)


def _kv_up_columns():
    per = C_NOPE_DIM + C_V_DIM
    keys, vals = [], []
    for h in range(C_HEADS):
        keys += [h * per + d for d in range(C_NOPE_DIM)] + [-1] * (LANES - C_NOPE_DIM)
        vals += [h * per + C_NOPE_DIM + d for d in range(C_V_DIM)]
    return np.asarray(keys + vals, np.int32)


def _rope_partner(t):
    lane = lax.broadcasted_iota(jnp.int32, t.shape, 1)
    ahead = pltpu.roll(t, LANES - ROPE_HALF, 1)
    behind = pltpu.roll(t, ROPE_HALF, 1)
    return jnp.where((lane & ROPE_HALF) == 0, ahead, behind)


def _half_rms(t, n):
    lane = lax.broadcasted_iota(jnp.int32, t.shape, 1)
    left = lane < HEAD_DIM
    sq = t * t
    s_l = jnp.sum(jnp.where(left, sq, 0.0), axis=-1, keepdims=True)
    s_r = jnp.sum(jnp.where(left, 0.0, sq), axis=-1, keepdims=True)
    return jnp.where(left, lax.rsqrt(s_l / n + EPS), lax.rsqrt(s_r / n + EPS))


def _proj_kernel(x_ref, m_ref, g_ref, w_ref, aqn_ref, akn_ref, cqn_ref, wq_ref, ckvn_ref, wkv_ref,
                 ac_ref, as_ref, t1_ref, t2_ref,
                 qa_ref, ka_ref, va_ref, qb_ref, kb_ref, vb_ref, qc_ref, kc_ref, vc_ref):
    x = x_ref[0]
    h = _norm_mod(x, g_ref[...], m_ref[0, 0], m_ref[0, 1]).astype(BF16)
    p = jnp.dot(h, w_ref[...], preferred_element_type=F32)
    seg = [p[:, _SEG_OFFS[i]:_SEG_OFFS[i + 1]] for i in range(len(_SEG_SIZES))]
    aq, ak, av, bq, bk, bv, cq, ckv, kr = seg

    cos_a = ac_ref[...]
    sin_a = as_ref[...]
    qa_scale = HEAD_DIM ** -0.5 * LOG2_E
    qb_scale = HEAD_DIM ** -0.5 * LOG2_E
    qc_scale = C_QK_DIM ** -0.5 * LOG2_E
    ones = jnp.ones((x.shape[0], LANES), BF16)
    gq_c = aqn_ref[0:1, :] * cos_a
    gq_s = aqn_ref[1:2, :] * sin_a
    for j in range(Q_SLOTS):
        sl = slice(j * LANES, (j + 1) * LANES)
        r = _half_rms(aq[:, sl], float(HEAD_DIM))
        qa_ref[0, :, sl] = ((aq[:, sl] * gq_c + _rope_partner(aq[:, sl]) * gq_s) * (r * qa_scale)).astype(BF16)
        qb_ref[0, :, sl] = (bq[:, sl] * qb_scale).astype(BF16)
    r = _half_rms(ak, float(HEAD_DIM))
    ka_ref[0] = ((ak * (akn_ref[0:1, :] * cos_a) + _rope_partner(ak) * (akn_ref[1:2, :] * sin_a)) * r).astype(BF16)
    ones_t = jnp.ones((VT_ONES, x.shape[0]), BF16)
    av_t = av.T
    for hh in range(A_KV_HEADS):
        va_ref[0, hh * VT_HEAD_H:hh * VT_HEAD_H + HEAD_DIM, :] = av_t[hh * HEAD_DIM:(hh + 1) * HEAD_DIM].astype(BF16)
        va_ref[0, hh * VT_HEAD_H + HEAD_DIM:(hh + 1) * VT_HEAD_H, :] = ones_t
    kb_ref[0] = bk.astype(BF16)
    vb_ref[0, :, :LANES] = bv.astype(BF16)
    vb_ref[0, :, LANES:] = ones

    t1 = t1_ref[...]
    t2 = t2_ref[...]
    cqn = cq * lax.rsqrt(jnp.mean(cq * cq, axis=-1, keepdims=True) + EPS) * cqn_ref[...]
    qu = jnp.dot(cqn.astype(BF16), wq_ref[...], preferred_element_type=F32)
    ckvn = ckv * lax.rsqrt(jnp.mean(ckv * ckv, axis=-1, keepdims=True) + EPS) * ckvn_ref[...]
    kvu = jnp.dot(ckvn.astype(BF16), wkv_ref[...], preferred_element_type=F32)
    k_rope = kr * t1 + _rope_partner(kr) * t2
    for hh in range(C_HEADS):
        sl = slice(hh * LANES, (hh + 1) * LANES)
        sl2 = slice(C_Q_W + hh * LANES, C_Q_W + (hh + 1) * LANES)
        qc_ref[0, :, sl] = ((qu[:, sl] * t1 + qu[:, sl2] * t2) * qc_scale).astype(BF16)
        kc_ref[0, :, sl] = (kvu[:, sl] + k_rope).astype(BF16)
    vc_t = kvu[:, C_Q_W:].T
    for hh in range(C_HEADS):
        vc_ref[0, hh * VT_HEAD_H:hh * VT_HEAD_H + C_V_DIM, :] = vc_t[hh * C_V_DIM:(hh + 1) * C_V_DIM].astype(BF16)
        vc_ref[0, hh * VT_HEAD_H + C_V_DIM:(hh + 1) * VT_HEAD_H, :] = ones_t


def _proj(x, mods, mod_idx, gain, w_attn, aqn, akn, cqn, wq, ckvn, wkv, tabs):
    b, s, d = x.shape
    row = lambda w: pl.BlockSpec((1, ROW_TILE, w), lambda i, j: (i, j, 0))
    col = lambda h: pl.BlockSpec((1, h, ROW_TILE), lambda i, j: (i, 0, j))
    tab = pl.BlockSpec((ROW_TILE, LANES), lambda i, j: (j, 0))
    outs = (("row", A_Q_W), ("row", A_KV_W), ("col", A_KV_HEADS * VT_HEAD_H), ("row", B_Q_W), ("row", B_KV_W),
            ("row", V_SLOT_W), ("row", C_Q_W), ("row", C_Q_W), ("col", C_HEADS * VT_HEAD_H))
    return pl.pallas_call(
        _proj_kernel,
        grid=(b, s // ROW_TILE),
        in_specs=[
            row(d),
            pl.BlockSpec((1, 3, 1, d), lambda i, j: (i, mod_idx, 0, 0)),
            _const_spec((1, d)),
            _const_spec((d, PROJ_W)),
            _const_spec((2, LANES)),
            _const_spec((2, LANES)),
            _const_spec((1, C_Q_RANK)),
            _const_spec((C_Q_RANK, 2 * C_Q_W)),
            _const_spec((1, C_KV_RANK)),
            _const_spec((C_KV_RANK, C_Q_W + C_V_W)),
            tab, tab, tab, tab,
        ],
        out_specs=[row(w) if kind == "row" else col(w) for kind, w in outs],
        out_shape=[jax.ShapeDtypeStruct((b, s, w) if kind == "row" else (b, w, s), BF16) for kind, w in outs],
        compiler_params=_params("arbitrary", "arbitrary"),
        name="attn_proj",
    )(x, mods, gain.reshape(1, d), w_attn, aqn, akn, cqn, wq, ckvn, wkv, *tabs)


def _lane_left(shape):
    return lax.broadcasted_iota(jnp.int32, shape, 1) < HEAD_DIM


def _dense_attn_steps(q_ref, k_ref, vt_ref, o_ref, s_new, m_new, s_old, m_old, *, plan):
    tq = q_ref.shape[1]
    n_keys = k_ref.shape[1]
    left = _lane_left((tq, LANES))
    col = 0
    pieces = {}
    for heads, k_slot, v_slot in plan:
        qs = []
        for q_slot, q_half, _, _ in heads:
            q = q_ref[0, :, q_slot * LANES:(q_slot + 1) * LANES]
            if q_half is not None:
                q = jnp.where(left if q_half == 0 else jnp.logical_not(left), q, jnp.zeros_like(q))
            qs.append(q)
        q = qs[0] if len(qs) == 1 else jnp.concatenate(qs, axis=0)
        cols = slice(col, col + q.shape[0])
        col += q.shape[0]
        v_half = heads[0][3]
        assert all(h[3] == v_half for h in heads)
        v_rows = slice(v_slot * VT_SLOT_H + v_half * VT_HEAD_H, v_slot * VT_SLOT_H + (v_half + 1) * VT_HEAD_H)
        m_prev = m_old[0:1, cols]
        m_run = o_run = None
        for c in range(n_keys // KEY_PIECE):
            keys = slice(c * KEY_PIECE, (c + 1) * KEY_PIECE)
            k = k_ref[0, keys, k_slot * LANES:(k_slot + 1) * LANES]
            s = lax.dot_general(k, q, (((1,), (1,)), ((), ())), preferred_element_type=F32)
            s_new[keys, cols] = s
            mx = jnp.max(s, axis=0, keepdims=True)
            m_run = mx if c == 0 else jnp.maximum(m_run, mx)

            p = jnp.exp2(s_old[keys, cols] - m_prev)
            pv = jnp.dot(vt_ref[0, v_rows, keys], p.astype(BF16), preferred_element_type=F32)
            o_run = pv if c == 0 else o_run + pv
        m_new[:, cols] = jnp.broadcast_to(m_run, (m_new.shape[0], q.shape[0]))
        o = o_run[:HEAD_DIM] * (1.0 / o_run[HEAD_DIM:HEAD_DIM + 1])
        for n, (_, _, out_slot, out_half) in enumerate(heads):
            pieces[(out_slot, out_half)] = o[:, n * tq:(n + 1) * tq]
    for j in range(o_ref.shape[2] // LANES):
        both = jnp.concatenate([pieces[(j, 0)], pieces[(j, 1)]], axis=0)
        o_ref[0, :, j * LANES:(j + 1) * LANES] = both.T.astype(BF16)


def _win_start(tile, nq, s_len):
    q0 = (tile % nq) * Q_TILE
    start = jnp.clip(q0 - WINDOW, 0, s_len - WIN_KEYS)
    place = (start - q0 + 2 * WINDOW) // WINDOW
    return pl.multiple_of(start, LANES), place


def _win_attn_steps(sink_ref, q_ref, k_ref, v_ref, tab_ref, o_ref, s_new, m_new, s_old, m_old, *,
                    nq, tiles):
    t = pl.program_id(0)
    s_len = k_ref.shape[1]
    start_new, place = _win_start(jnp.minimum(t, tiles - 1), nq, s_len)
    start_old, _ = _win_start(jnp.maximum(t - 1, 0), nq, s_len)
    kw = k_ref[0, pl.ds(start_new, WIN_KEYS), :]
    vw = v_ref[0, pl.ds(start_old, WIN_KEYS), :]
    left = _lane_left((Q_TILE, LANES))
    g = B_HEADS // B_KV_HEADS
    slots = [q_ref[0, :, j * LANES:(j + 1) * LANES] for j in range(Q_SLOTS)]
    halves = []
    for side in range(B_KV_HEADS):
        keep = left if side == 0 else jnp.logical_not(left)
        q = jnp.concatenate([jnp.where(keep, qs, jnp.zeros_like(qs)) for qs in slots], axis=0)
        s = lax.dot_general(q, kw, (((1,), (1,)), ((), ())), preferred_element_type=F32)
        rows = slice(side * g * Q_TILE, (side + 1) * g * Q_TILE)
        sinks = [sink_ref[side * g + j] * LOG2_E for j in range(g)]
        sb, ms = [], []
        for j in range(g):
            sj = s[j * Q_TILE:(j + 1) * Q_TILE] + tab_ref[place, side * g + j]
            sb.append(sj)
            ms.append(jnp.maximum(jnp.max(sj, axis=-1, keepdims=True), sinks[j]))
        s_new[rows, :] = jnp.concatenate(sb, axis=0)
        m_new[rows, :] = jnp.concatenate(ms, axis=0)

        m_prev = m_old[rows, :]
        p = jnp.exp2(s_old[rows, :] - m_prev)
        pv = jnp.dot(p.astype(BF16), vw, preferred_element_type=F32)
        outs = []
        for j in range(g):
            hr = slice(j * Q_TILE, (j + 1) * Q_TILE)
            l = pv[hr, LANES:] + jnp.exp2(sinks[j] - m_prev[hr])
            outs.append(pv[hr, :LANES] * (1.0 / l))
        halves.append(outs)
    for j in range(Q_SLOTS):
        o_ref[0, :, j * LANES:(j + 1) * LANES] = jnp.where(left, halves[0][j], halves[1][j]).astype(BF16)


def _zero_init(t, refs):
    @pl.when(t == 0)
    def _():
        for ref in refs:
            ref[...] = jnp.zeros(ref.shape, F32)


def _dense_attn_kernel(q_ref, k_ref, v_ref, o_ref, s0, m0, s1, m1, *, plan):
    t = pl.program_id(0)
    step = functools.partial(_dense_attn_steps, q_ref, k_ref, v_ref, o_ref, plan=plan)
    _zero_init(t, (s1, m1))

    @pl.when(t % 2 == 0)
    def _():
        step(s0, m0, s1, m1)

    @pl.when(t % 2 == 1)
    def _():
        step(s1, m1, s0, m0)


def _attn_ab_kernel(sink_ref, qa_ref, ka_ref, va_ref, qb_ref, kb_ref, vb_ref, tab_ref, oa_ref, ob_ref,
                    sa0, ma0, sa1, ma1, sb0, mb0, sb1, mb1, *, plan, nq, tiles):
    t = pl.program_id(0)

    def step(sa_new, ma_new, sa_old, ma_old, sb_new, mb_new, sb_old, mb_old):
        _win_attn_steps(sink_ref, qb_ref, kb_ref, vb_ref, tab_ref, ob_ref, sb_new, mb_new, sb_old, mb_old,
                        nq=nq, tiles=tiles)
        _dense_attn_steps(qa_ref, ka_ref, va_ref, oa_ref, sa_new, ma_new, sa_old, ma_old, plan=plan)

    _zero_init(t, (sa1, ma1, sb1, mb1))

    @pl.when(t % 2 == 0)
    def _():
        step(sa0, ma0, sa1, ma1, sb0, mb0, sb1, mb1)

    @pl.when(t % 2 == 1)
    def _():
        step(sa1, ma1, sa0, ma0, sb1, mb1, sb0, mb0)


def _pipeline_specs(b, s, q_tile):
    nq = s // q_tile
    tiles = b * nq

    def cur(t):
        return jnp.minimum(t, tiles - 1)

    def prev(t):
        return jnp.maximum(t - 1, 0)

    q_spec = lambda w: pl.BlockSpec((1, q_tile, w), lambda t: (cur(t) // nq, cur(t) % nq, 0))
    k_spec = lambda w: pl.BlockSpec((1, s, w), lambda t: (cur(t) // nq, 0, 0))
    v_spec = lambda w: pl.BlockSpec((1, s, w), lambda t: (prev(t) // nq, 0, 0))
    o_spec = lambda w: pl.BlockSpec((1, q_tile, w), lambda t: (prev(t) // nq, prev(t) % nq, 0))
    vt_spec = lambda h: pl.BlockSpec((1, h, s), lambda t: (prev(t) // nq, 0, 0))
    return nq, tiles, q_spec, k_spec, v_spec, vt_spec, o_spec


def _score_scratch(rows, keys):
    return [pltpu.VMEM(shape, F32) for _ in range(2) for shape in ((rows, keys), (rows, 1))]


def _score_scratch_t(keys, cols):
    return [pltpu.VMEM(shape, F32) for _ in range(2) for shape in ((keys, cols), (8, cols))]


def _dense_attn(q, k, v, plan, q_tile, name):
    b, s, qw = q.shape
    ow = (max(h[2] for g in plan for h in g[0]) + 1) * LANES
    stacked_rows = sum(len(g[0]) for g in plan) * q_tile
    _, tiles, q_spec, k_spec, _, vt_spec, o_spec = _pipeline_specs(b, s, q_tile)
    return pl.pallas_call(
        functools.partial(_dense_attn_kernel, plan=plan),
        grid=(tiles + 1,),
        in_specs=[q_spec(qw), k_spec(k.shape[2]), vt_spec(v.shape[1])],
        out_specs=o_spec(ow),
        out_shape=jax.ShapeDtypeStruct((b, s, ow), BF16),
        scratch_shapes=_score_scratch_t(s, stacked_rows),
        compiler_params=_params("arbitrary"),
        name=name,
    )(q, k, v)


def _attn_ab(qa, ka, va, qb, kb, vb, tab, sink, plan):
    b, s, qw = qa.shape
    rows_a = sum(len(g[0]) for g in plan) * Q_TILE
    nq, tiles, q_spec, k_spec, v_spec, vt_spec, o_spec = _pipeline_specs(b, s, Q_TILE)
    return pl.pallas_call(
        functools.partial(_attn_ab_kernel, plan=plan, nq=nq, tiles=tiles),
        grid=(tiles + 1,),
        in_specs=[
            pl.BlockSpec(memory_space=pltpu.SMEM),
            q_spec(qw), k_spec(ka.shape[2]), vt_spec(va.shape[1]),
            q_spec(qb.shape[2]), k_spec(kb.shape[2]), v_spec(vb.shape[2]),
            _const_spec(tab.shape),
        ],
        out_specs=[o_spec(qw), o_spec(qb.shape[2])],
        out_shape=[jax.ShapeDtypeStruct((b, s, qw), BF16), jax.ShapeDtypeStruct((b, s, qb.shape[2]), BF16)],
        scratch_shapes=_score_scratch_t(s, rows_a) + _score_scratch(B_HEADS * Q_TILE, WIN_KEYS),
        compiler_params=_params("arbitrary"),
        name="attn_ab",
    )(sink, qa, ka, va, qb, kb, vb, tab)


def _bias_tab_kernel(rb_ref, idx_ref, o_ref):
    for p in range(WIN_PLACEMENTS):
        idx = idx_ref[p]
        for h in range(B_HEADS):
            acc = jnp.full(idx.shape, NEG_INF, F32)
            for bkt in range(NUM_BUCKETS):
                acc = jnp.where(idx == bkt, rb_ref[bkt, h] * LOG2_E, acc)
            o_ref[p, h] = acc


def _t5_bucket(rel):
    nb = NUM_BUCKETS // 2
    max_exact = nb // 2
    ret = jnp.where(rel > 0, nb, 0)
    n = jnp.abs(rel)
    large = max_exact + (jnp.log(jnp.maximum(n, 1).astype(jnp.float32) / max_exact)
                         / math.log(MAX_DISTANCE / max_exact) * (nb - max_exact)).astype(jnp.int32)
    large = jnp.minimum(large, nb - 1)
    return ret + jnp.where(n < max_exact, n, large)


def _bias_table(rel_bias):
    p = jnp.arange(WIN_PLACEMENTS)[:, None, None]
    r = jnp.arange(Q_TILE)[None, :, None]
    j = jnp.arange(WIN_KEYS)[None, None, :]
    rel = j + p * WINDOW - 2 * WINDOW - r
    idx = jnp.where(jnp.abs(rel) <= WINDOW, _t5_bucket(rel), -1).astype(jnp.int32)
    shape = (WIN_PLACEMENTS, B_HEADS, Q_TILE, WIN_KEYS)
    return pl.pallas_call(
        _bias_tab_kernel,
        in_specs=[pl.BlockSpec(memory_space=pltpu.SMEM),
                  pl.BlockSpec(idx.shape, lambda: (0, 0, 0))],
        out_specs=pl.BlockSpec(shape, lambda: (0, 0, 0, 0)),
        out_shape=jax.ShapeDtypeStruct(shape, F32),
        compiler_params=pltpu.CompilerParams(vmem_limit_bytes=V7X_VMEM_LIMIT_BYTES),
        name="win_bias_table",
    )(rel_bias, idx)


def _mix_out_kernel(x_ref, m_ref, g_ref, oa_ref, ob_ref, oc_ref, wg_ref, wa_ref, wb_ref, wc_ref,
                    wo_ref, o_ref):
    x = x_ref[0]
    d = x.shape[-1]
    h = _norm_mod(x, g_ref[...], m_ref[0, 0], m_ref[0, 1]).astype(BF16)
    gates = jnp.dot(h, wg_ref[...], preferred_element_type=F32)
    ya = jnp.dot(oa_ref[0], wa_ref[...], preferred_element_type=F32)
    yb = jnp.dot(ob_ref[0], wb_ref[...], preferred_element_type=F32)
    yc = jnp.dot(oc_ref[0], wc_ref[...], preferred_element_type=F32)
    merged = (_sigmoid(gates[:, :d]) * ya + _sigmoid(gates[:, d:2 * d]) * yb
              + _sigmoid(gates[:, 2 * d:]) * yc)
    out = jnp.dot(merged.astype(BF16), wo_ref[...], preferred_element_type=F32)
    o_ref[0] = x + m_ref[0, 2] * out


def _mix_out(x, mods, mod_idx, gain, o_a, o_b, o_c, w_gates, w_a, w_b, w_c, w_o):
    b, s, d = x.shape
    row = lambda w: pl.BlockSpec((1, ROW_TILE, w), lambda i, j: (i, j, 0))
    return pl.pallas_call(
        _mix_out_kernel,
        grid=(b, s // ROW_TILE),
        in_specs=[
            row(d),
            pl.BlockSpec((1, 3, 1, d), lambda i, j: (i, mod_idx, 0, 0)),
            _const_spec((1, d)),
            row(o_a.shape[2]), row(o_b.shape[2]), row(o_c.shape[2]),
            _const_spec(w_gates.shape), _const_spec(w_a.shape), _const_spec(w_b.shape),
            _const_spec(w_c.shape), _const_spec(w_o.shape),
        ],
        out_specs=row(d),
        out_shape=jax.ShapeDtypeStruct((b, s, d), F32),
        compiler_params=_params("arbitrary", "arbitrary"),
        name="mix_out",
    )(x, mods, gain.reshape(1, d), o_a, o_b, o_c, w_gates, w_a, w_b, w_c, w_o)


def _rope_angles(pos, dim):
    inv = ROPE_THETA ** (-jnp.arange(0, dim, 2, dtype=jnp.float32) / dim)
    ang = pos.astype(jnp.float32)[:, None] * inv[None, :]
    return jnp.cos(ang), jnp.sin(ang)


def _rope_tables(s):
    rows = s // GRID_W
    t = jnp.arange(s)
    row_pos = jnp.repeat(jnp.arange(rows), GRID_W)
    col_pos = jnp.tile(jnp.arange(GRID_W), rows)
    cr, sr = _rope_angles(row_pos, HEAD_DIM // 2)
    cc, sc = _rope_angles(col_pos, HEAD_DIM // 2)
    cos_h = jnp.concatenate([cr, cr, cc, cc], axis=-1)
    sin_h = jnp.concatenate([-sr, sr, -sc, sc], axis=-1)
    reps = LANES // HEAD_DIM
    cos_a = jnp.tile(cos_h, (1, reps))
    sin_a = jnp.tile(sin_h, (1, reps))
    cs, ss = _rope_angles(t, C_ROPE_DIM)
    ones = jnp.ones((s, C_NOPE_DIM), F32)
    zeros_lo = jnp.zeros((s, C_NOPE_DIM), F32)
    zeros_hi = jnp.zeros((s, LANES - C_QK_DIM), F32)
    t1 = jnp.concatenate([ones, cs, cs, zeros_hi], axis=-1)
    t2 = jnp.concatenate([zeros_lo, -ss, ss, zeros_hi], axis=-1)
    return cos_a, sin_a, t1, t2


def _attn_plans():
    plan_a = tuple((tuple((j, g, j, g) for j in range(Q_SLOTS)), 0, 0) for g in range(A_KV_HEADS))
    plan_c = tuple((((h, None, h // 2, h % 2),), h, h // 2) for h in range(C_HEADS))
    return plan_a, plan_c


def kernel(x, c, ada_w, ada_b, norm_ffn1, ffn1_w_gu, ffn1_w_down, norm_mix, w_in, a_q_norm, a_k_norm,
           b_sink, rel_bias, c_q_lat_norm, c_w_q_up, c_kv_lat_norm, c_w_kv_up, w_br_a, w_br_b, w_br_c,
           w_out, norm_ffn2, ffn2_w_gu, ffn2_w_down, final_norm):
    b, s, d = x.shape
    assert d == D_MODEL and s % max(Q_TILE, C_Q_TILE, ROW_TILE) == 0 and s >= WIN_KEYS and s % GRID_W == 0
    depth = ada_w.shape[0]

    mods = _ada_mods(c, ada_w, ada_b)
    tabs = _rope_tables(s)
    bias_tab = _bias_table(rel_bias)
    plan_a, plan_c = _attn_plans()

    proj_cols = _proj_columns()
    q_up_cols = _q_up_columns()
    kv_up_cols = _kv_up_columns()
    head_rows = np.asarray([h * HEAD_DIM + dd for h in _q_head_order() for dd in range(HEAD_DIM)])
    partner = np.arange(HEAD_DIM) ^ ROPE_HALF
    reps = LANES // HEAD_DIM

    for l in range(depth):
        x = _ffn(x, mods, 3 * l, norm_ffn1[l], ffn1_w_gu[l], ffn1_w_down[l])

        w_attn = _take_cols(w_in[l], proj_cols).astype(BF16)
        aqn = jnp.stack([jnp.tile(a_q_norm[l], reps), jnp.tile(a_q_norm[l][partner], reps)])
        akn = jnp.stack([jnp.tile(a_k_norm[l], reps), jnp.tile(a_k_norm[l][partner], reps)])
        wq = _take_cols(c_w_q_up[l], q_up_cols).astype(BF16)
        wkv = _take_cols(c_w_kv_up[l], kv_up_cols).astype(BF16)
        qa, ka, va, qb, kb, vb, qc, kc, vc = _proj(
            x, mods, 3 * l + 1, norm_mix[l], w_attn, aqn, akn,
            c_q_lat_norm[l].reshape(1, -1), wq, c_kv_lat_norm[l].reshape(1, -1), wkv, tabs)

        o_a, o_b = _attn_ab(qa, ka, va, qb, kb, vb, bias_tab, b_sink[l], plan_a)
        o_c = _dense_attn(qc, kc, vc, plan_c, C_Q_TILE, "attn_c")

        x = _mix_out(
            x, mods, 3 * l + 1, norm_mix[l], o_a, o_b, o_c,
            w_in[l][:, GATE_OFF:].astype(BF16),
            w_br_a[l][head_rows].astype(BF16), w_br_b[l][head_rows].astype(BF16),
            w_br_c[l].astype(BF16), w_out[l].astype(BF16))

        x = _ffn(x, mods, 3 * l + 2, norm_ffn2[l], ffn2_w_gu[l], ffn2_w_down[l],
                 final_gain=final_norm if l == depth - 1 else None)
    return x
```

```python
import functools
import math

import numpy as np
import jax
import jax.numpy as jnp
from jax import lax
from jax.experimental import pallas as pl
from jax.experimental.pallas import tpu as pltpu

D_MODEL = 1024
DEPTH = 2
HEAD_DIM = 64
A_HEADS = 6
A_KV_HEADS = 2
B_HEADS = 6
B_KV_HEADS = 2
C_HEADS = 4
C_Q_RANK = 256
C_KV_RANK = 128
C_NOPE_DIM = 64
C_ROPE_DIM = 32
C_V_DIM = 64
C_QK_DIM = C_NOPE_DIM + C_ROPE_DIM
D_FF = 2816
GRID_W = 64
WINDOW = 128
NUM_BUCKETS = 32
MAX_DISTANCE = 128
ROPE_THETA = 10000.0
ADA_CHUNKS = 9
EPS = 1e-6
NEG_INF = -1e30

A_Q_W = A_HEADS * HEAD_DIM
A_KV_W = A_KV_HEADS * HEAD_DIM
B_Q_W = B_HEADS * HEAD_DIM
B_KV_W = B_KV_HEADS * HEAD_DIM
IN_SIZES = (A_Q_W, A_KV_W, A_KV_W, B_Q_W, B_KV_W, B_KV_W,
            C_Q_RANK, C_KV_RANK, C_ROPE_DIM, D_MODEL, D_MODEL, D_MODEL)
IN_OFFS = tuple(int(v) for v in np.cumsum((0,) + IN_SIZES))
GATE_OFF = IN_OFFS[9]
ROPE_HALF = HEAD_DIM // 4
assert ROPE_HALF == C_ROPE_DIM // 2

LANES = 128
MXU_TILE = 256
V7X_VMEM_LIMIT_BYTES = 60000 * 1024

V_SLOT_W = 2 * LANES
VT_ONES = 16
VT_HEAD_H = HEAD_DIM + VT_ONES
VT_SLOT_H = 2 * VT_HEAD_H
LOG2_E = math.log2(math.e)
ROW_TILE = 1024
FFN_CHUNK = MXU_TILE
Q_TILE = 256
C_Q_TILE = 512
KEY_PIECE = 512
WIN_KEYS = Q_TILE + 2 * WINDOW
WIN_PLACEMENTS = 3

BF16 = jnp.bfloat16
F32 = jnp.float32


def _params(*sem):
    return pltpu.CompilerParams(dimension_semantics=sem, vmem_limit_bytes=V7X_VMEM_LIMIT_BYTES)


def _const_spec(shape):
    nd = len(shape)
    return pl.BlockSpec(shape, lambda *_: (0,) * nd, pipeline_mode=pl.Buffered(1))


def _sigmoid(v):
    return 1.0 / (1.0 + jnp.exp(-v))


def _norm_mod(x, gain, shift, scale):
    ms = jnp.mean(x * x, axis=-1, keepdims=True)
    y = x * lax.rsqrt(ms + EPS) * gain
    return y * (1.0 + scale) + shift


def _ada_kernel(c_ref, w_ref, b_ref, o_ref):
    c = c_ref[...]
    cond = (c * _sigmoid(c)).astype(BF16)
    o_ref[0] = jnp.dot(cond, w_ref[0].astype(BF16), preferred_element_type=F32) + b_ref[0]


def _ada_mods(c, ada_w, ada_b):
    depth, d, _ = ada_w.shape
    b = c.shape[0]
    bias = ada_b.reshape(depth * ADA_CHUNKS, 1, d)
    out = pl.pallas_call(
        _ada_kernel,
        grid=(depth, ADA_CHUNKS),
        in_specs=[
            pl.BlockSpec((b, d), lambda l, j: (0, 0)),
            pl.BlockSpec((1, d, d), lambda l, j: (l, 0, j)),
            pl.BlockSpec((1, 1, d), lambda l, j: (l * ADA_CHUNKS + j, 0, 0)),
        ],
        out_specs=pl.BlockSpec((1, b, d), lambda l, j: (l * ADA_CHUNKS + j, 0, 0)),
        out_shape=jax.ShapeDtypeStruct((depth * ADA_CHUNKS, b, d), F32),
        compiler_params=_params("arbitrary", "arbitrary"),
        name="ada_mods",
    )(c, ada_w, bias)
    return jnp.transpose(out, (1, 0, 2))[:, :, None, :]


def _ffn_kernel(x_ref, m_ref, g_ref, wgu_ref, wd_ref, *rest, final):
    o_ref = rest[-1]
    x = x_ref[0]
    h = _norm_mod(x, g_ref[...], m_ref[0, 0], m_ref[0, 1]).astype(BF16)
    y = None
    for c in range(D_FF // FFN_CHUNK):
        gate = jnp.dot(h, wgu_ref[:, c * FFN_CHUNK:(c + 1) * FFN_CHUNK], preferred_element_type=F32)
        up = jnp.dot(h, wgu_ref[:, D_FF + c * FFN_CHUNK:D_FF + (c + 1) * FFN_CHUNK], preferred_element_type=F32)
        act = (gate * _sigmoid(gate) * up).astype(BF16)
        yc = jnp.dot(act, wd_ref[c * FFN_CHUNK:(c + 1) * FFN_CHUNK, :], preferred_element_type=F32)
        y = yc if c == 0 else y + yc
    out = x + (0.5 * m_ref[0, 2]) * y
    if final:
        ms = jnp.mean(out * out, axis=-1, keepdims=True)
        out = out * lax.rsqrt(ms + EPS) * rest[0][...]
    o_ref[0] = out


def _ffn(x, mods, mod_idx, gain, w_gu, w_down, final_gain=None):
    b, s, d = x.shape
    final = final_gain is not None
    in_specs = [
        pl.BlockSpec((1, ROW_TILE, d), lambda i, j: (i, j, 0)),
        pl.BlockSpec((1, 3, 1, d), lambda i, j: (i, mod_idx, 0, 0)),
        _const_spec((1, d)),
        _const_spec((d, 2 * D_FF)),
        _const_spec((D_FF, d)),
    ]
    args = [x, mods, gain.reshape(1, d), w_gu.astype(BF16), w_down.astype(BF16)]
    if final:
        in_specs.append(_const_spec((1, d)))
        args.append(final_gain.reshape(1, d))
    return pl.pallas_call(
        functools.partial(_ffn_kernel, final=final),
        grid=(b, s // ROW_TILE),
        in_specs=in_specs,
        out_specs=pl.BlockSpec((1, ROW_TILE, d), lambda i, j: (i, j, 0)),
        out_shape=jax.ShapeDtypeStruct((b, s, d), F32),
        compiler_params=_params("arbitrary", "arbitrary"),
        name="ffn_final" if final else "ffn",
    )(*args)


_SEG_SIZES = (A_Q_W, A_KV_W, A_KV_W, B_Q_W, B_KV_W, B_KV_W, C_Q_RANK, C_KV_RANK, LANES)
_SEG_OFFS = tuple(int(v) for v in np.cumsum((0,) + _SEG_SIZES))
PROJ_W = _SEG_OFFS[-1]
Q_SLOTS = A_HEADS // A_KV_HEADS
C_Q_W = C_HEADS * LANES
C_V_W = C_HEADS * C_V_DIM


def _q_head_order():
    g = A_HEADS // A_KV_HEADS
    return [k * g + j for j in range(g) for k in range(A_KV_HEADS)]


def _proj_columns():
    aq, ak, av, bq, bk, bv, cq, ckv, ckr = IN_OFFS[:9]
    order = _q_head_order()
    cols = []
    cols += [aq + h * HEAD_DIM + d for h in order for d in range(HEAD_DIM)]
    cols += list(range(ak, ak + A_KV_W))
    cols += list(range(av, av + A_KV_W))
    cols += [bq + h * HEAD_DIM + d for h in order for d in range(HEAD_DIM)]
    cols += list(range(bk, bk + B_KV_W))
    cols += list(range(bv, bv + B_KV_W))
    cols += list(range(cq, cq + C_Q_RANK))
    cols += list(range(ckv, ckv + C_KV_RANK))
    cols += [-1] * C_NOPE_DIM + [ckr + d for d in range(C_ROPE_DIM)] + [-1] * (LANES - C_QK_DIM)
    cols = np.asarray(cols, np.int32)
    assert cols.shape[0] == PROJ_W
    return cols


def _take_cols(w, cols):
    picked = jnp.take(w, jnp.asarray(np.maximum(cols, 0)), axis=1)
    return jnp.where(jnp.asarray(cols >= 0)[None, :], picked, 0.0)


def _q_up_columns():
    rdims = np.arange(C_ROPE_DIM)
    rpartner = rdims ^ ROPE_HALF
    pad_hi = LANES - C_QK_DIM
    direct, part = [], []
    for h in range(C_HEADS):
        base = h * C_QK_DIM
        direct += [base + d for d in range(C_NOPE_DIM)] + [base + C_NOPE_DIM + d for d in rdims] + [-1] * pad_hi
        part += [-1] * C_NOPE_DIM + [base + C_NOPE_DIM + d for d in rpartner] + [-1] * pad_hi
    return np.asarray(direct + part, np.int32)


def _kv_up_columns():
    per = C_NOPE_DIM + C_V_DIM
    keys, vals = [], []
    for h in range(C_HEADS):
        keys += [h * per + d for d in range(C_NOPE_DIM)] + [-1] * (LANES - C_NOPE_DIM)
        vals += [h * per + C_NOPE_DIM + d for d in range(C_V_DIM)]
    return np.asarray(keys + vals, np.int32)


def _rope_partner(t):
    lane = lax.broadcasted_iota(jnp.int32, t.shape, 1)
    ahead = pltpu.roll(t, LANES - ROPE_HALF, 1)
    behind = pltpu.roll(t, ROPE_HALF, 1)
    return jnp.where((lane & ROPE_HALF) == 0, ahead, behind)


def _half_rms(t, n):
    lane = lax.broadcasted_iota(jnp.int32, t.shape, 1)
    left = lane < HEAD_DIM
    sq = t * t
    s_l = jnp.sum(jnp.where(left, sq, 0.0), axis=-1, keepdims=True)
    s_r = jnp.sum(jnp.where(left, 0.0, sq), axis=-1, keepdims=True)
    return jnp.where(left, lax.rsqrt(s_l / n + EPS), lax.rsqrt(s_r / n + EPS))


def _proj_kernel(x_ref, m_ref, g_ref, w_ref, aqn_ref, akn_ref, cqn_ref, wq_ref, ckvn_ref, wkv_ref,
                 ac_ref, as_ref, t1_ref, t2_ref,
                 qa_ref, ka_ref, va_ref, qb_ref, kb_ref, vb_ref, qc_ref, kc_ref, vc_ref):
    x = x_ref[0]
    h = _norm_mod(x, g_ref[...], m_ref[0, 0], m_ref[0, 1]).astype(BF16)
    p = jnp.dot(h, w_ref[...], preferred_element_type=F32)
    seg = [p[:, _SEG_OFFS[i]:_SEG_OFFS[i + 1]] for i in range(len(_SEG_SIZES))]
    aq, ak, av, bq, bk, bv, cq, ckv, kr = seg

    cos_a = ac_ref[...]
    sin_a = as_ref[...]
    qa_scale = HEAD_DIM ** -0.5 * LOG2_E
    qb_scale = HEAD_DIM ** -0.5 * LOG2_E
    qc_scale = C_QK_DIM ** -0.5 * LOG2_E
    ones = jnp.ones((x.shape[0], LANES), BF16)
    gq_c = aqn_ref[0:1, :] * cos_a
    gq_s = aqn_ref[1:2, :] * sin_a
    for j in range(Q_SLOTS):
        sl = slice(j * LANES, (j + 1) * LANES)
        r = _half_rms(aq[:, sl], float(HEAD_DIM))
        qa_ref[0, :, sl] = ((aq[:, sl] * gq_c + _rope_partner(aq[:, sl]) * gq_s) * (r * qa_scale)).astype(BF16)
        qb_ref[0, :, sl] = (bq[:, sl] * qb_scale).astype(BF16)
    r = _half_rms(ak, float(HEAD_DIM))
    ka_ref[0] = ((ak * (akn_ref[0:1, :] * cos_a) + _rope_partner(ak) * (akn_ref[1:2, :] * sin_a)) * r).astype(BF16)
    ones_t = jnp.ones((VT_ONES, x.shape[0]), BF16)
    av_t = av.T
    for hh in range(A_KV_HEADS):
        va_ref[0, hh * VT_HEAD_H:hh * VT_HEAD_H + HEAD_DIM, :] = av_t[hh * HEAD_DIM:(hh + 1) * HEAD_DIM].astype(BF16)
        va_ref[0, hh * VT_HEAD_H + HEAD_DIM:(hh + 1) * VT_HEAD_H, :] = ones_t
    kb_ref[0] = bk.astype(BF16)
    vb_ref[0, :, :LANES] = bv.astype(BF16)
    vb_ref[0, :, LANES:] = ones

    t1 = t1_ref[...]
    t2 = t2_ref[...]
    cqn = cq * lax.rsqrt(jnp.mean(cq * cq, axis=-1, keepdims=True) + EPS) * cqn_ref[...]
    qu = jnp.dot(cqn.astype(BF16), wq_ref[...], preferred_element_type=F32)
    ckvn = ckv * lax.rsqrt(jnp.mean(ckv * ckv, axis=-1, keepdims=True) + EPS) * ckvn_ref[...]
    kvu = jnp.dot(ckvn.astype(BF16), wkv_ref[...], preferred_element_type=F32)
    k_rope = kr * t1 + _rope_partner(kr) * t2
    for hh in range(C_HEADS):
        sl = slice(hh * LANES, (hh + 1) * LANES)
        sl2 = slice(C_Q_W + hh * LANES, C_Q_W + (hh + 1) * LANES)
        qc_ref[0, :, sl] = ((qu[:, sl] * t1 + qu[:, sl2] * t2) * qc_scale).astype(BF16)
        kc_ref[0, :, sl] = (kvu[:, sl] + k_rope).astype(BF16)
    vc_t = kvu[:, C_Q_W:].T
    for hh in range(C_HEADS):
        vc_ref[0, hh * VT_HEAD_H:hh * VT_HEAD_H + C_V_DIM, :] = vc_t[hh * C_V_DIM:(hh + 1) * C_V_DIM].astype(BF16)
        vc_ref[0, hh * VT_HEAD_H + C_V_DIM:(hh + 1) * VT_HEAD_H, :] = ones_t


def _proj(x, mods, mod_idx, gain, w_attn, aqn, akn, cqn, wq, ckvn, wkv, tabs):
    b, s, d = x.shape
    row = lambda w: pl.BlockSpec((1, ROW_TILE, w), lambda i, j: (i, j, 0))
    col = lambda h: pl.BlockSpec((1, h, ROW_TILE), lambda i, j: (i, 0, j))
    tab = pl.BlockSpec((ROW_TILE, LANES), lambda i, j: (j, 0))
    outs = (("row", A_Q_W), ("row", A_KV_W), ("col", A_KV_HEADS * VT_HEAD_H), ("row", B_Q_W), ("row", B_KV_W),
            ("row", V_SLOT_W), ("row", C_Q_W), ("row", C_Q_W), ("col", C_HEADS * VT_HEAD_H))
    return pl.pallas_call(
        _proj_kernel,
        grid=(b, s // ROW_TILE),
        in_specs=[
            row(d),
            pl.BlockSpec((1, 3, 1, d), lambda i, j: (i, mod_idx, 0, 0)),
            _const_spec((1, d)),
            _const_spec((d, PROJ_W)),
            _const_spec((2, LANES)),
            _const_spec((2, LANES)),
            _const_spec((1, C_Q_RANK)),
            _const_spec((C_Q_RANK, 2 * C_Q_W)),
            _const_spec((1, C_KV_RANK)),
            _const_spec((C_KV_RANK, C_Q_W + C_V_W)),
            tab, tab, tab, tab,
        ],
        out_specs=[row(w) if kind == "row" else col(w) for kind, w in outs],
        out_shape=[jax.ShapeDtypeStruct((b, s, w) if kind == "row" else (b, w, s), BF16) for kind, w in outs],
        compiler_params=_params("arbitrary", "arbitrary"),
        name="attn_proj",
    )(x, mods, gain.reshape(1, d), w_attn, aqn, akn, cqn, wq, ckvn, wkv, *tabs)


def _lane_left(shape):
    return lax.broadcasted_iota(jnp.int32, shape, 1) < HEAD_DIM


def _dense_attn_steps(q_ref, k_ref, vt_ref, o_ref, s_new, m_new, s_old, m_old, *, plan, score_keys):
    tq = q_ref.shape[1]
    n_keys = k_ref.shape[1]
    left = _lane_left((tq, LANES))
    col = 0
    pieces = {}
    for heads, k_slot, v_slot in plan:
        qs = []
        for q_slot, q_half, _, _ in heads:
            q = q_ref[0, :, q_slot * LANES:(q_slot + 1) * LANES]
            if q_half is not None:
                q = jnp.where(left if q_half == 0 else jnp.logical_not(left), q, jnp.zeros_like(q))
            qs.append(q)
        q = qs[0] if len(qs) == 1 else jnp.concatenate(qs, axis=0)
        cols = slice(col, col + q.shape[0])
        col += q.shape[0]
        v_half = heads[0][3]
        assert all(h[3] == v_half for h in heads)
        v_rows = slice(v_slot * VT_SLOT_H + v_half * VT_HEAD_H, v_slot * VT_SLOT_H + (v_half + 1) * VT_HEAD_H)
        m_prev = m_old[0:1, cols]
        m_run = o_run = None
        for c in range(n_keys // KEY_PIECE):
            if c * KEY_PIECE % score_keys == 0:
                keys = slice(c * KEY_PIECE, c * KEY_PIECE + score_keys)
                k = k_ref[0, keys, k_slot * LANES:(k_slot + 1) * LANES]
                s = lax.dot_general(k, q, (((1,), (1,)), ((), ())), preferred_element_type=F32)
                s_new[keys, cols] = s
                mx = jnp.max(s, axis=0, keepdims=True)
                m_run = mx if c == 0 else jnp.maximum(m_run, mx)

            keys = slice(c * KEY_PIECE, (c + 1) * KEY_PIECE)
            p = jnp.exp2(s_old[keys, cols] - m_prev)
            pv = jnp.dot(vt_ref[0, v_rows, keys], p.astype(BF16), preferred_element_type=F32)
            o_run = pv if c == 0 else o_run + pv
        m_new[:, cols] = jnp.broadcast_to(m_run, (m_new.shape[0], q.shape[0]))
        o = o_run[:HEAD_DIM] * (1.0 / o_run[HEAD_DIM:HEAD_DIM + 1])
        for n, (_, _, out_slot, out_half) in enumerate(heads):
            pieces[(out_slot, out_half)] = o[:, n * tq:(n + 1) * tq]
    for j in range(o_ref.shape[2] // LANES):
        both = jnp.concatenate([pieces[(j, 0)], pieces[(j, 1)]], axis=0)
        o_ref[0, :, j * LANES:(j + 1) * LANES] = both.T.astype(BF16)


def _win_start(tile, nq, s_len):
    q0 = (tile % nq) * Q_TILE
    start = jnp.clip(q0 - WINDOW, 0, s_len - WIN_KEYS)
    place = (start - q0 + 2 * WINDOW) // WINDOW
    return pl.multiple_of(start, LANES), place


def _win_attn_steps(sink_ref, q_ref, k_ref, v_ref, tab_ref, o_ref, s_new, m_new, s_old, m_old, *,
                    nq, tiles):
    t = pl.program_id(0)
    s_len = k_ref.shape[1]
    start_new, place = _win_start(jnp.minimum(t, tiles - 1), nq, s_len)
    start_old, _ = _win_start(jnp.maximum(t - 1, 0), nq, s_len)
    kw = k_ref[0, pl.ds(start_new, WIN_KEYS), :]
    vw = v_ref[0, pl.ds(start_old, WIN_KEYS), :]
    left = _lane_left((Q_TILE, LANES))
    g = B_HEADS // B_KV_HEADS
    slots = [q_ref[0, :, j * LANES:(j + 1) * LANES] for j in range(Q_SLOTS)]
    halves = []
    for side in range(B_KV_HEADS):
        keep = left if side == 0 else jnp.logical_not(left)
        q = jnp.concatenate([jnp.where(keep, qs, jnp.zeros_like(qs)) for qs in slots], axis=0)
        s = lax.dot_general(q, kw, (((1,), (1,)), ((), ())), preferred_element_type=F32)
        rows = slice(side * g * Q_TILE, (side + 1) * g * Q_TILE)
        sinks = [sink_ref[side * g + j] * LOG2_E for j in range(g)]
        sb, ms = [], []
        for j in range(g):
            sj = s[j * Q_TILE:(j + 1) * Q_TILE] + tab_ref[place, side * g + j]
            sb.append(sj)
            ms.append(jnp.maximum(jnp.max(sj, axis=-1, keepdims=True), sinks[j]))
        s_new[rows, :] = jnp.concatenate(sb, axis=0)
        m_new[rows, :] = jnp.concatenate(ms, axis=0)

        m_prev = m_old[rows, :]
        p = jnp.exp2(s_old[rows, :] - m_prev)
        pv = jnp.dot(p.astype(BF16), vw, preferred_element_type=F32)
        outs = []
        for j in range(g):
            hr = slice(j * Q_TILE, (j + 1) * Q_TILE)
            l = pv[hr, LANES:] + jnp.exp2(sinks[j] - m_prev[hr])
            outs.append(pv[hr, :LANES] * (1.0 / l))
        halves.append(outs)
    for j in range(Q_SLOTS):
        o_ref[0, :, j * LANES:(j + 1) * LANES] = jnp.where(left, halves[0][j], halves[1][j]).astype(BF16)


def _zero_init(t, refs):
    @pl.when(t == 0)
    def _():
        for ref in refs:
            ref[...] = jnp.zeros(ref.shape, F32)


def _dense_attn_kernel(q_ref, k_ref, v_ref, o_ref, s0, m0, s1, m1, *, plan):
    t = pl.program_id(0)
    step = functools.partial(_dense_attn_steps, q_ref, k_ref, v_ref, o_ref, plan=plan, score_keys=KEY_PIECE)
    _zero_init(t, (s1, m1))

    @pl.when(t % 2 == 0)
    def _():
        step(s0, m0, s1, m1)

    @pl.when(t % 2 == 1)
    def _():
        step(s1, m1, s0, m0)


def _attn_ab_kernel(sink_ref, qa_ref, ka_ref, va_ref, qb_ref, kb_ref, vb_ref, tab_ref, oa_ref, ob_ref,
                    sa0, ma0, sa1, ma1, sb0, mb0, sb1, mb1, *, plan, nq, tiles):
    t = pl.program_id(0)

    def step(sa_new, ma_new, sa_old, ma_old, sb_new, mb_new, sb_old, mb_old):
        _win_attn_steps(sink_ref, qb_ref, kb_ref, vb_ref, tab_ref, ob_ref, sb_new, mb_new, sb_old, mb_old,
                        nq=nq, tiles=tiles)
        _dense_attn_steps(qa_ref, ka_ref, va_ref, oa_ref, sa_new, ma_new, sa_old, ma_old, plan=plan,
                          score_keys=ka_ref.shape[1])

    _zero_init(t, (sa1, ma1, sb1, mb1))

    @pl.when(t % 2 == 0)
    def _():
        step(sa0, ma0, sa1, ma1, sb0, mb0, sb1, mb1)

    @pl.when(t % 2 == 1)
    def _():
        step(sa1, ma1, sa0, ma0, sb1, mb1, sb0, mb0)


def _pipeline_specs(b, s, q_tile):
    nq = s // q_tile
    tiles = b * nq

    def cur(t):
        return jnp.minimum(t, tiles - 1)

    def prev(t):
        return jnp.maximum(t - 1, 0)

    q_spec = lambda w: pl.BlockSpec((1, q_tile, w), lambda t: (cur(t) // nq, cur(t) % nq, 0))
    k_spec = lambda w: pl.BlockSpec((1, s, w), lambda t: (cur(t) // nq, 0, 0))
    v_spec = lambda w: pl.BlockSpec((1, s, w), lambda t: (prev(t) // nq, 0, 0))
    o_spec = lambda w: pl.BlockSpec((1, q_tile, w), lambda t: (prev(t) // nq, prev(t) % nq, 0))
    vt_spec = lambda h: pl.BlockSpec((1, h, s), lambda t: (prev(t) // nq, 0, 0))
    return nq, tiles, q_spec, k_spec, v_spec, vt_spec, o_spec


def _score_scratch(rows, keys):
    return [pltpu.VMEM(shape, F32) for _ in range(2) for shape in ((rows, keys), (rows, 1))]


def _score_scratch_t(keys, cols):
    return [pltpu.VMEM(shape, F32) for _ in range(2) for shape in ((keys, cols), (8, cols))]


def _dense_attn(q, k, v, plan, q_tile, name):
    b, s, qw = q.shape
    ow = (max(h[2] for g in plan for h in g[0]) + 1) * LANES
    stacked_rows = sum(len(g[0]) for g in plan) * q_tile
    _, tiles, q_spec, k_spec, _, vt_spec, o_spec = _pipeline_specs(b, s, q_tile)
    return pl.pallas_call(
        functools.partial(_dense_attn_kernel, plan=plan),
        grid=(tiles + 1,),
        in_specs=[q_spec(qw), k_spec(k.shape[2]), vt_spec(v.shape[1])],
        out_specs=o_spec(ow),
        out_shape=jax.ShapeDtypeStruct((b, s, ow), BF16),
        scratch_shapes=_score_scratch_t(s, stacked_rows),
        compiler_params=_params("arbitrary"),
        name=name,
    )(q, k, v)


def _attn_ab(qa, ka, va, qb, kb, vb, tab, sink, plan):
    b, s, qw = qa.shape
    rows_a = sum(len(g[0]) for g in plan) * Q_TILE
    nq, tiles, q_spec, k_spec, v_spec, vt_spec, o_spec = _pipeline_specs(b, s, Q_TILE)
    return pl.pallas_call(
        functools.partial(_attn_ab_kernel, plan=plan, nq=nq, tiles=tiles),
        grid=(tiles + 1,),
        in_specs=[
            pl.BlockSpec(memory_space=pltpu.SMEM),
            q_spec(qw), k_spec(ka.shape[2]), vt_spec(va.shape[1]),
            q_spec(qb.shape[2]), k_spec(kb.shape[2]), v_spec(vb.shape[2]),
            _const_spec(tab.shape),
        ],
        out_specs=[o_spec(qw), o_spec(qb.shape[2])],
        out_shape=[jax.ShapeDtypeStruct((b, s, qw), BF16), jax.ShapeDtypeStruct((b, s, qb.shape[2]), BF16)],
        scratch_shapes=_score_scratch_t(s, rows_a) + _score_scratch(B_HEADS * Q_TILE, WIN_KEYS),
        compiler_params=_params("arbitrary"),
        name="attn_ab",
    )(sink, qa, ka, va, qb, kb, vb, tab)


def _bias_tab_kernel(rb_ref, idx_ref, o_ref):
    for p in range(WIN_PLACEMENTS):
        idx = idx_ref[p]
        for h in range(B_HEADS):
            acc = jnp.full(idx.shape, NEG_INF, F32)
            for bkt in range(NUM_BUCKETS):
                acc = jnp.where(idx == bkt, rb_ref[bkt, h] * LOG2_E, acc)
            o_ref[p, h] = acc


def _t5_bucket(rel):
    nb = NUM_BUCKETS // 2
    max_exact = nb // 2
    ret = jnp.where(rel > 0, nb, 0)
    n = jnp.abs(rel)
    large = max_exact + (jnp.log(jnp.maximum(n, 1).astype(jnp.float32) / max_exact)
                         / math.log(MAX_DISTANCE / max_exact) * (nb - max_exact)).astype(jnp.int32)
    large = jnp.minimum(large, nb - 1)
    return ret + jnp.where(n < max_exact, n, large)


def _bias_table(rel_bias):
    p = jnp.arange(WIN_PLACEMENTS)[:, None, None]
    r = jnp.arange(Q_TILE)[None, :, None]
    j = jnp.arange(WIN_KEYS)[None, None, :]
    rel = j + p * WINDOW - 2 * WINDOW - r
    idx = jnp.where(jnp.abs(rel) <= WINDOW, _t5_bucket(rel), -1).astype(jnp.int32)
    shape = (WIN_PLACEMENTS, B_HEADS, Q_TILE, WIN_KEYS)
    return pl.pallas_call(
        _bias_tab_kernel,
        in_specs=[pl.BlockSpec(memory_space=pltpu.SMEM),
                  pl.BlockSpec(idx.shape, lambda: (0, 0, 0))],
        out_specs=pl.BlockSpec(shape, lambda: (0, 0, 0, 0)),
        out_shape=jax.ShapeDtypeStruct(shape, F32),
        compiler_params=pltpu.CompilerParams(vmem_limit_bytes=V7X_VMEM_LIMIT_BYTES),
        name="win_bias_table",
    )(rel_bias, idx)


def _mix_out_kernel(x_ref, m_ref, g_ref, oa_ref, ob_ref, oc_ref, wg_ref, wa_ref, wb_ref, wc_ref,
                    wo_ref, o_ref):
    x = x_ref[0]
    d = x.shape[-1]
    h = _norm_mod(x, g_ref[...], m_ref[0, 0], m_ref[0, 1]).astype(BF16)
    gates = jnp.dot(h, wg_ref[...], preferred_element_type=F32)
    ya = jnp.dot(oa_ref[0], wa_ref[...], preferred_element_type=F32)
    yb = jnp.dot(ob_ref[0], wb_ref[...], preferred_element_type=F32)
    yc = jnp.dot(oc_ref[0], wc_ref[...], preferred_element_type=F32)
    merged = (_sigmoid(gates[:, :d]) * ya + _sigmoid(gates[:, d:2 * d]) * yb
              + _sigmoid(gates[:, 2 * d:]) * yc)
    out = jnp.dot(merged.astype(BF16), wo_ref[...], preferred_element_type=F32)
    o_ref[0] = x + m_ref[0, 2] * out


def _mix_out(x, mods, mod_idx, gain, o_a, o_b, o_c, w_gates, w_a, w_b, w_c, w_o):
    b, s, d = x.shape
    row = lambda w: pl.BlockSpec((1, ROW_TILE, w), lambda i, j: (i, j, 0))
    return pl.pallas_call(
        _mix_out_kernel,
        grid=(b, s // ROW_TILE),
        in_specs=[
            row(d),
            pl.BlockSpec((1, 3, 1, d), lambda i, j: (i, mod_idx, 0, 0)),
            _const_spec((1, d)),
            row(o_a.shape[2]), row(o_b.shape[2]), row(o_c.shape[2]),
            _const_spec(w_gates.shape), _const_spec(w_a.shape), _const_spec(w_b.shape),
            _const_spec(w_c.shape), _const_spec(w_o.shape),
        ],
        out_specs=row(d),
        out_shape=jax.ShapeDtypeStruct((b, s, d), F32),
        compiler_params=_params("arbitrary", "arbitrary"),
        name="mix_out",
    )(x, mods, gain.reshape(1, d), o_a, o_b, o_c, w_gates, w_a, w_b, w_c, w_o)


def _rope_angles(pos, dim):
    inv = ROPE_THETA ** (-jnp.arange(0, dim, 2, dtype=jnp.float32) / dim)
    ang = pos.astype(jnp.float32)[:, None] * inv[None, :]
    return jnp.cos(ang), jnp.sin(ang)


def _rope_tables(s):
    rows = s // GRID_W
    t = jnp.arange(s)
    row_pos = jnp.repeat(jnp.arange(rows), GRID_W)
    col_pos = jnp.tile(jnp.arange(GRID_W), rows)
    cr, sr = _rope_angles(row_pos, HEAD_DIM // 2)
    cc, sc = _rope_angles(col_pos, HEAD_DIM // 2)
    cos_h = jnp.concatenate([cr, cr, cc, cc], axis=-1)
    sin_h = jnp.concatenate([-sr, sr, -sc, sc], axis=-1)
    reps = LANES // HEAD_DIM
    cos_a = jnp.tile(cos_h, (1, reps))
    sin_a = jnp.tile(sin_h, (1, reps))
    cs, ss = _rope_angles(t, C_ROPE_DIM)
    ones = jnp.ones((s, C_NOPE_DIM), F32)
    zeros_lo = jnp.zeros((s, C_NOPE_DIM), F32)
    zeros_hi = jnp.zeros((s, LANES - C_QK_DIM), F32)
    t1 = jnp.concatenate([ones, cs, cs, zeros_hi], axis=-1)
    t2 = jnp.concatenate([zeros_lo, -ss, ss, zeros_hi], axis=-1)
    return cos_a, sin_a, t1, t2


def _attn_plans():
    plan_a = tuple((tuple((j, g, j, g) for j in range(Q_SLOTS)), 0, 0) for g in range(A_KV_HEADS))
    plan_c = tuple((((h, None, h // 2, h % 2),), h, h // 2) for h in range(C_HEADS))
    return plan_a, plan_c


def kernel(x, c, ada_w, ada_b, norm_ffn1, ffn1_w_gu, ffn1_w_down, norm_mix, w_in, a_q_norm, a_k_norm,
           b_sink, rel_bias, c_q_lat_norm, c_w_q_up, c_kv_lat_norm, c_w_kv_up, w_br_a, w_br_b, w_br_c,
           w_out, norm_ffn2, ffn2_w_gu, ffn2_w_down, final_norm):
    b, s, d = x.shape
    assert d == D_MODEL and s % max(Q_TILE, C_Q_TILE, ROW_TILE) == 0 and s >= WIN_KEYS and s % GRID_W == 0
    depth = ada_w.shape[0]

    mods = _ada_mods(c, ada_w, ada_b)
    tabs = _rope_tables(s)
    bias_tab = _bias_table(rel_bias)
    plan_a, plan_c = _attn_plans()

    proj_cols = _proj_columns()
    q_up_cols = _q_up_columns()
    kv_up_cols = _kv_up_columns()
    head_rows = np.asarray([h * HEAD_DIM + dd for h in _q_head_order() for dd in range(HEAD_DIM)])
    partner = np.arange(HEAD_DIM) ^ ROPE_HALF
    reps = LANES // HEAD_DIM

    for l in range(depth):
        x = _ffn(x, mods, 3 * l, norm_ffn1[l], ffn1_w_gu[l], ffn1_w_down[l])

        w_attn = _take_cols(w_in[l], proj_cols).astype(BF16)
        aqn = jnp.stack([jnp.tile(a_q_norm[l], reps), jnp.tile(a_q_norm[l][partner], reps)])
        akn = jnp.stack([jnp.tile(a_k_norm[l], reps), jnp.tile(a_k_norm[l][partner], reps)])
        wq = _take_cols(c_w_q_up[l], q_up_cols).astype(BF16)
        wkv = _take_cols(c_w_kv_up[l], kv_up_cols).astype(BF16)
        qa, ka, va, qb, kb, vb, qc, kc, vc = _proj(
            x, mods, 3 * l + 1, norm_mix[l], w_attn, aqn, akn,
            c_q_lat_norm[l].reshape(1, -1), wq, c_kv_lat_norm[l].reshape(1, -1), wkv, tabs)

        o_a, o_b = _attn_ab(qa, ka, va, qb, kb, vb, bias_tab, b_sink[l], plan_a)
        o_c = _dense_attn(qc, kc, vc, plan_c, C_Q_TILE, "attn_c")

        x = _mix_out(
            x, mods, 3 * l + 1, norm_mix[l], o_a, o_b, o_c,
            w_in[l][:, GATE_OFF:].astype(BF16),
            w_br_a[l][head_rows].astype(BF16), w_br_b[l][head_rows].astype(BF16),
            w_br_c[l].astype(BF16), w_out[l].astype(BF16))

        x = _ffn(x, mods, 3 * l + 2, norm_ffn2[l], ffn2_w_gu[l], ffn2_w_down[l],
                 final_gain=final_norm if l == depth - 1 else None)
    return x
```

```python
import functools
import math

import numpy as np
import jax
import jax.numpy as jnp
from jax import lax
from jax.experimental import pallas as pl
from jax.experimental.pallas import tpu as pltpu

D_MODEL = 1024
DEPTH = 2
HEAD_DIM = 64
A_HEADS = 6
A_KV_HEADS = 2
B_HEADS = 6
B_KV_HEADS = 2
C_HEADS = 4
C_Q_RANK = 256
C_KV_RANK = 128
C_NOPE_DIM = 64
C_ROPE_DIM = 32
C_V_DIM = 64
C_QK_DIM = C_NOPE_DIM + C_ROPE_DIM
D_FF = 2816
GRID_W = 64
WINDOW = 128
NUM_BUCKETS = 32
MAX_DISTANCE = 128
ROPE_THETA = 10000.0
ADA_CHUNKS = 9
EPS = 1e-6
NEG_INF = -1e30

A_Q_W = A_HEADS * HEAD_DIM
A_KV_W = A_KV_HEADS * HEAD_DIM
B_Q_W = B_HEADS * HEAD_DIM
B_KV_W = B_KV_HEADS * HEAD_DIM
IN_SIZES = (A_Q_W, A_KV_W, A_KV_W, B_Q_W, B_KV_W, B_KV_W,
            C_Q_RANK, C_KV_RANK, C_ROPE_DIM, D_MODEL, D_MODEL, D_MODEL)
IN_OFFS = tuple(int(v) for v in np.cumsum((0,) + IN_SIZES))
GATE_OFF = IN_OFFS[9]
ROPE_HALF = HEAD_DIM // 4
assert ROPE_HALF == C_ROPE_DIM // 2

LANES = 128
MXU_TILE = 256
V7X_VMEM_LIMIT_BYTES = 60000 * 1024

VT_ONES = 16
VT_HEAD_H = HEAD_DIM + VT_ONES
VT_SLOT_H = 2 * VT_HEAD_H
LOG2_E = math.log2(math.e)
ROW_TILE = 1024
FFN_CHUNK = MXU_TILE
Q_TILE = 256
C_Q_TILE = 512
KEY_PIECE = 512
WIN_KEYS = Q_TILE + 2 * WINDOW
WIN_PLACEMENTS = 3

BF16 = jnp.bfloat16
F32 = jnp.float32


def _params(*sem):
    return pltpu.CompilerParams(dimension_semantics=sem, vmem_limit_bytes=V7X_VMEM_LIMIT_BYTES)


def _const_spec(shape):
    nd = len(shape)
    return pl.BlockSpec(shape, lambda *_: (0,) * nd, pipeline_mode=pl.Buffered(1))


def _sigmoid(v):
    return 1.0 / (1.0 + jnp.exp(-v))


def _norm_mod(x, gain, shift, scale):
    ms = jnp.mean(x * x, axis=-1, keepdims=True)
    y = x * lax.rsqrt(ms + EPS) * gain
    return y * (1.0 + scale) + shift


def _ada_kernel(c_ref, w_ref, b_ref, o_ref):
    c = c_ref[...]
    cond = (c * _sigmoid(c)).astype(BF16)
    o_ref[0] = jnp.dot(cond, w_ref[0].astype(BF16), preferred_element_type=F32) + b_ref[0]


def _ada_mods(c, ada_w, ada_b):
    depth, d, _ = ada_w.shape
    b = c.shape[0]
    bias = ada_b.reshape(depth * ADA_CHUNKS, 1, d)
    out = pl.pallas_call(
        _ada_kernel,
        grid=(depth, ADA_CHUNKS),
        in_specs=[
            pl.BlockSpec((b, d), lambda l, j: (0, 0)),
            pl.BlockSpec((1, d, d), lambda l, j: (l, 0, j)),
            pl.BlockSpec((1, 1, d), lambda l, j: (l * ADA_CHUNKS + j, 0, 0)),
        ],
        out_specs=pl.BlockSpec((1, b, d), lambda l, j: (l * ADA_CHUNKS + j, 0, 0)),
        out_shape=jax.ShapeDtypeStruct((depth * ADA_CHUNKS, b, d), F32),
        compiler_params=_params("arbitrary", "arbitrary"),
        name="ada_mods",
    )(c, ada_w, bias)
    return jnp.transpose(out, (1, 0, 2))[:, :, None, :]


def _ffn_kernel(x_ref, m_ref, g_ref, wgu_ref, wd_ref, *rest, final):
    o_ref = rest[-1]
    x = x_ref[0]
    h = _norm_mod(x, g_ref[...], m_ref[0, 0], m_ref[0, 1]).astype(BF16)
    y = None
    for c in range(D_FF // FFN_CHUNK):
        gate = jnp.dot(h, wgu_ref[:, c * FFN_CHUNK:(c + 1) * FFN_CHUNK], preferred_element_type=F32)
        up = jnp.dot(h, wgu_ref[:, D_FF + c * FFN_CHUNK:D_FF + (c + 1) * FFN_CHUNK], preferred_element_type=F32)
        act = (gate * _sigmoid(gate) * up).astype(BF16)
        yc = jnp.dot(act, wd_ref[c * FFN_CHUNK:(c + 1) * FFN_CHUNK, :], preferred_element_type=F32)
        y = yc if c == 0 else y + yc
    out = x + (0.5 * m_ref[0, 2]) * y
    if final:
        ms = jnp.mean(out * out, axis=-1, keepdims=True)
        out = out * lax.rsqrt(ms + EPS) * rest[0][...]
    o_ref[0] = out


def _ffn(x, mods, mod_idx, gain, w_gu, w_down, final_gain=None):
    b, s, d = x.shape
    final = final_gain is not None
    in_specs = [
        pl.BlockSpec((1, ROW_TILE, d), lambda i, j: (i, j, 0)),
        pl.BlockSpec((1, 3, 1, d), lambda i, j: (i, mod_idx, 0, 0)),
        _const_spec((1, d)),
        _const_spec((d, 2 * D_FF)),
        _const_spec((D_FF, d)),
    ]
    args = [x, mods, gain.reshape(1, d), w_gu.astype(BF16), w_down.astype(BF16)]
    if final:
        in_specs.append(_const_spec((1, d)))
        args.append(final_gain.reshape(1, d))
    return pl.pallas_call(
        functools.partial(_ffn_kernel, final=final),
        grid=(b, s // ROW_TILE),
        in_specs=in_specs,
        out_specs=pl.BlockSpec((1, ROW_TILE, d), lambda i, j: (i, j, 0)),
        out_shape=jax.ShapeDtypeStruct((b, s, d), F32),
        compiler_params=_params("arbitrary", "arbitrary"),
        name="ffn_final" if final else "ffn",
    )(*args)


_SEG_SIZES = (A_Q_W, A_KV_W, A_KV_W, B_Q_W, B_KV_W, B_KV_W, C_Q_RANK, C_KV_RANK, LANES)
_SEG_OFFS = tuple(int(v) for v in np.cumsum((0,) + _SEG_SIZES))
PROJ_W = _SEG_OFFS[-1]
Q_SLOTS = A_HEADS // A_KV_HEADS
C_Q_W = C_HEADS * LANES
C_V_W = C_HEADS * C_V_DIM


def _q_head_order():
    g = A_HEADS // A_KV_HEADS
    return [k * g + j for j in range(g) for k in range(A_KV_HEADS)]


def _proj_columns():
    aq, ak, av, bq, bk, bv, cq, ckv, ckr = IN_OFFS[:9]
    order = _q_head_order()
    cols = []
    cols += [aq + h * HEAD_DIM + d for h in order for d in range(HEAD_DIM)]
    cols += list(range(ak, ak + A_KV_W))
    cols += list(range(av, av + A_KV_W))
    cols += [bq + h * HEAD_DIM + d for h in order for d in range(HEAD_DIM)]
    cols += list(range(bk, bk + B_KV_W))
    cols += list(range(bv, bv + B_KV_W))
    cols += list(range(cq, cq + C_Q_RANK))
    cols += list(range(ckv, ckv + C_KV_RANK))
    cols += [-1] * C_NOPE_DIM + [ckr + d for d in range(C_ROPE_DIM)] + [-1] * (LANES - C_QK_DIM)
    cols = np.asarray(cols, np.int32)
    assert cols.shape[0] == PROJ_W
    return cols


def _take_cols(w, cols):
    picked = jnp.take(w, jnp.asarray(np.maximum(cols, 0)), axis=1)
    return jnp.where(jnp.asarray(cols >= 0)[None, :], picked, 0.0)


def _q_up_columns():
    rdims = np.arange(C_ROPE_DIM)
    rpartner = rdims ^ ROPE_HALF
    pad_hi = LANES - C_QK_DIM
    direct, part = [], []
    for h in range(C_HEADS):
        base = h * C_QK_DIM
        direct += [base + d for d in range(C_NOPE_DIM)] + [base + C_NOPE_DIM + d for d in rdims] + [-1] * pad_hi
        part += [-1] * C_NOPE_DIM + [base + C_NOPE_DIM + d for d in rpartner] + [-1] * pad_hi
    return np.asarray(direct + part, np.int32)


def _kv_up_columns():
    per = C_NOPE_DIM + C_V_DIM
    keys, vals = [], []
    for h in range(C_HEADS):
        keys += [h * per + d for d in range(C_NOPE_DIM)] + [-1] * (LANES - C_NOPE_DIM)
        vals += [h * per + C_NOPE_DIM + d for d in range(C_V_DIM)]
    return np.asarray(keys + vals, np.int32)


def _rope_partner(t):
    lane = lax.broadcasted_iota(jnp.int32, t.shape, 1)
    ahead = pltpu.roll(t, LANES - ROPE_HALF, 1)
    behind = pltpu.roll(t, ROPE_HALF, 1)
    return jnp.where((lane & ROPE_HALF) == 0, ahead, behind)


def _half_rms(t, n):
    lane = lax.broadcasted_iota(jnp.int32, t.shape, 1)
    left = lane < HEAD_DIM
    sq = t * t
    s_l = jnp.sum(jnp.where(left, sq, 0.0), axis=-1, keepdims=True)
    s_r = jnp.sum(jnp.where(left, 0.0, sq), axis=-1, keepdims=True)
    return jnp.where(left, lax.rsqrt(s_l / n + EPS), lax.rsqrt(s_r / n + EPS))


def _proj_kernel(x_ref, m_ref, g_ref, w_ref, aqn_ref, akn_ref, cqn_ref, wq_ref, ckvn_ref, wkv_ref,
                 ac_ref, as_ref, t1_ref, t2_ref,
                 qa_ref, ka_ref, va_ref, qb_ref, kb_ref, vb_ref, qc_ref, kc_ref, vc_ref):
    x = x_ref[0]
    h = _norm_mod(x, g_ref[...], m_ref[0, 0], m_ref[0, 1]).astype(BF16)
    p = jnp.dot(h, w_ref[...], preferred_element_type=F32)
    seg = [p[:, _SEG_OFFS[i]:_SEG_OFFS[i + 1]] for i in range(len(_SEG_SIZES))]
    aq, ak, av, bq, bk, bv, cq, ckv, kr = seg

    cos_a = ac_ref[...]
    sin_a = as_ref[...]
    qa_scale = HEAD_DIM ** -0.5 * LOG2_E
    qb_scale = HEAD_DIM ** -0.5 * LOG2_E
    qc_scale = C_QK_DIM ** -0.5 * LOG2_E
    gq_c = aqn_ref[0:1, :] * cos_a
    gq_s = aqn_ref[1:2, :] * sin_a
    for j in range(Q_SLOTS):
        sl = slice(j * LANES, (j + 1) * LANES)
        r = _half_rms(aq[:, sl], float(HEAD_DIM))
        qa_ref[0, :, sl] = ((aq[:, sl] * gq_c + _rope_partner(aq[:, sl]) * gq_s) * (r * qa_scale)).astype(BF16)
        qb_ref[0, :, sl] = (bq[:, sl] * qb_scale).astype(BF16)
    r = _half_rms(ak, float(HEAD_DIM))
    ka_ref[0] = ((ak * (akn_ref[0:1, :] * cos_a) + _rope_partner(ak) * (akn_ref[1:2, :] * sin_a)) * r).astype(BF16)
    ones_t = jnp.ones((VT_ONES, x.shape[0]), BF16)
    av_t = av.T
    for hh in range(A_KV_HEADS):
        va_ref[0, hh * VT_HEAD_H:hh * VT_HEAD_H + HEAD_DIM, :] = av_t[hh * HEAD_DIM:(hh + 1) * HEAD_DIM].astype(BF16)
        va_ref[0, hh * VT_HEAD_H + HEAD_DIM:(hh + 1) * VT_HEAD_H, :] = ones_t
    kb_ref[0] = bk.astype(BF16)
    vb_ref[0] = bv.astype(BF16)

    t1 = t1_ref[...]
    t2 = t2_ref[...]
    cqn = cq * lax.rsqrt(jnp.mean(cq * cq, axis=-1, keepdims=True) + EPS) * cqn_ref[...]
    qu = jnp.dot(cqn.astype(BF16), wq_ref[...], preferred_element_type=F32)
    ckvn = ckv * lax.rsqrt(jnp.mean(ckv * ckv, axis=-1, keepdims=True) + EPS) * ckvn_ref[...]
    kvu = jnp.dot(ckvn.astype(BF16), wkv_ref[...], preferred_element_type=F32)
    k_rope = kr * t1 + _rope_partner(kr) * t2
    for hh in range(C_HEADS):
        sl = slice(hh * LANES, (hh + 1) * LANES)
        sl2 = slice(C_Q_W + hh * LANES, C_Q_W + (hh + 1) * LANES)
        qc_ref[0, :, sl] = ((qu[:, sl] * t1 + qu[:, sl2] * t2) * qc_scale).astype(BF16)
        kc_ref[0, :, sl] = (kvu[:, sl] + k_rope).astype(BF16)
    vc_t = kvu[:, C_Q_W:].T
    for hh in range(C_HEADS):
        vc_ref[0, hh * VT_HEAD_H:hh * VT_HEAD_H + C_V_DIM, :] = vc_t[hh * C_V_DIM:(hh + 1) * C_V_DIM].astype(BF16)
        vc_ref[0, hh * VT_HEAD_H + C_V_DIM:(hh + 1) * VT_HEAD_H, :] = ones_t


def _proj(x, mods, mod_idx, gain, w_attn, aqn, akn, cqn, wq, ckvn, wkv, tabs):
    b, s, d = x.shape
    row = lambda w: pl.BlockSpec((1, ROW_TILE, w), lambda i, j: (i, j, 0))
    col = lambda h: pl.BlockSpec((1, h, ROW_TILE), lambda i, j: (i, 0, j))
    tab = pl.BlockSpec((ROW_TILE, LANES), lambda i, j: (j, 0))
    outs = (("row", A_Q_W), ("row", A_KV_W), ("col", A_KV_HEADS * VT_HEAD_H), ("row", B_Q_W), ("row", B_KV_W),
            ("row", B_KV_W), ("row", C_Q_W), ("row", C_Q_W), ("col", C_HEADS * VT_HEAD_H))
    return pl.pallas_call(
        _proj_kernel,
        grid=(b, s // ROW_TILE),
        in_specs=[
            row(d),
            pl.BlockSpec((1, 3, 1, d), lambda i, j: (i, mod_idx, 0, 0)),
            _const_spec((1, d)),
            _const_spec((d, PROJ_W)),
            _const_spec((2, LANES)),
            _const_spec((2, LANES)),
            _const_spec((1, C_Q_RANK)),
            _const_spec((C_Q_RANK, 2 * C_Q_W)),
            _const_spec((1, C_KV_RANK)),
            _const_spec((C_KV_RANK, C_Q_W + C_V_W)),
            tab, tab, tab, tab,
        ],
        out_specs=[row(w) if kind == "row" else col(w) for kind, w in outs],
        out_shape=[jax.ShapeDtypeStruct((b, s, w) if kind == "row" else (b, w, s), BF16) for kind, w in outs],
        compiler_params=_params("arbitrary", "arbitrary"),
        name="attn_proj",
    )(x, mods, gain.reshape(1, d), w_attn, aqn, akn, cqn, wq, ckvn, wkv, *tabs)


def _lane_left(shape):
    return lax.broadcasted_iota(jnp.int32, shape, 1) < HEAD_DIM


def _dense_attn_steps(q_ref, k_ref, vt_ref, o_ref, s_new, m_new, s_old, m_old, *, plan, score_keys):
    tq = q_ref.shape[1]
    n_keys = k_ref.shape[1]
    left = _lane_left((tq, LANES))
    col = 0
    pieces = {}
    for heads, k_slot, v_slot in plan:
        qs = []
        for q_slot, q_half, _, _ in heads:
            q = q_ref[0, :, q_slot * LANES:(q_slot + 1) * LANES]
            if q_half is not None:
                q = jnp.where(left if q_half == 0 else jnp.logical_not(left), q, jnp.zeros_like(q))
            qs.append(q)
        q = qs[0] if len(qs) == 1 else jnp.concatenate(qs, axis=0)
        cols = slice(col, col + q.shape[0])
        col += q.shape[0]
        v_half = heads[0][3]
        assert all(h[3] == v_half for h in heads)
        v_rows = slice(v_slot * VT_SLOT_H + v_half * VT_HEAD_H, v_slot * VT_SLOT_H + (v_half + 1) * VT_HEAD_H)
        m_prev = m_old[0:1, cols]
        m_run = o_run = None
        for c in range(n_keys // KEY_PIECE):
            if c * KEY_PIECE % score_keys == 0:
                keys = slice(c * KEY_PIECE, c * KEY_PIECE + score_keys)
                k = k_ref[0, keys, k_slot * LANES:(k_slot + 1) * LANES]
                s = lax.dot_general(k, q, (((1,), (1,)), ((), ())), preferred_element_type=F32)
                s_new[keys, cols] = s
                mx = jnp.max(s, axis=0, keepdims=True)
                m_run = mx if c == 0 else jnp.maximum(m_run, mx)

            keys = slice(c * KEY_PIECE, (c + 1) * KEY_PIECE)
            p = jnp.exp2(s_old[keys, cols] - m_prev)
            pv = jnp.dot(vt_ref[0, v_rows, keys], p.astype(BF16), preferred_element_type=F32)
            o_run = pv if c == 0 else o_run + pv
        m_new[:, cols] = jnp.broadcast_to(m_run, (m_new.shape[0], q.shape[0]))
        o = o_run[:HEAD_DIM] * (1.0 / o_run[HEAD_DIM:HEAD_DIM + 1])
        for n, (_, _, out_slot, out_half) in enumerate(heads):
            pieces[(out_slot, out_half)] = o[:, n * tq:(n + 1) * tq]
    for j in range(o_ref.shape[2] // LANES):
        both = jnp.concatenate([pieces[(j, 0)], pieces[(j, 1)]], axis=0)
        o_ref[0, :, j * LANES:(j + 1) * LANES] = both.T.astype(BF16)


def _win_start(tile, nq, s_len):
    q0 = (tile % nq) * Q_TILE
    start = jnp.clip(q0 - WINDOW, 0, s_len - WIN_KEYS)
    place = (start - q0 + 2 * WINDOW) // WINDOW
    return pl.multiple_of(start, LANES), place


def _win_attn_steps(sink_ref, q_ref, k_ref, v_ref, tab_ref, o_ref, s_new, m_new, s_old, m_old, *,
                    nq, tiles):
    t = pl.program_id(0)
    s_len = k_ref.shape[1]
    start_new, place = _win_start(jnp.minimum(t, tiles - 1), nq, s_len)
    start_old, _ = _win_start(jnp.maximum(t - 1, 0), nq, s_len)
    kw = k_ref[0, pl.ds(start_new, WIN_KEYS), :]
    vw_t = v_ref[0, pl.ds(start_old, WIN_KEYS), :].astype(F32).T.astype(BF16)
    ones_t = jnp.ones((VT_ONES, WIN_KEYS), BF16)
    left = _lane_left((Q_TILE, LANES))
    g = B_HEADS // B_KV_HEADS
    slots = [q_ref[0, :, j * LANES:(j + 1) * LANES] for j in range(Q_SLOTS)]
    pieces = {}
    for side in range(B_KV_HEADS):
        keep = left if side == 0 else jnp.logical_not(left)
        q = jnp.concatenate([jnp.where(keep, qs, jnp.zeros_like(qs)) for qs in slots], axis=0)
        s = lax.dot_general(kw, q, (((1,), (1,)), ((), ())), preferred_element_type=F32)
        cols = slice(side * g * Q_TILE, (side + 1) * g * Q_TILE)
        sinks = [sink_ref[side * g + j] * LOG2_E for j in range(g)]
        sb, ms = [], []
        for j in range(g):
            sj = s[:, j * Q_TILE:(j + 1) * Q_TILE] + tab_ref[place, side * g + j]
            sb.append(sj)
            ms.append(jnp.maximum(jnp.max(sj, axis=0, keepdims=True), sinks[j]))
        s_new[:, cols] = jnp.concatenate(sb, axis=1)
        m_new[:, cols] = jnp.broadcast_to(jnp.concatenate(ms, axis=1), (m_new.shape[0], g * Q_TILE))

        m_prev = m_old[0:1, cols]
        p = jnp.exp2(s_old[:, cols] - m_prev)
        vt = jnp.concatenate([vw_t[side * HEAD_DIM:(side + 1) * HEAD_DIM], ones_t], axis=0)
        pv = jnp.dot(vt, p.astype(BF16), preferred_element_type=F32)
        for j in range(g):
            hc = slice(j * Q_TILE, (j + 1) * Q_TILE)
            l = pv[HEAD_DIM:HEAD_DIM + 1, hc] + jnp.exp2(sinks[j] - m_prev[:, hc])
            pieces[(j, side)] = pv[:HEAD_DIM, hc] * (1.0 / l)
    for j in range(Q_SLOTS):
        both = jnp.concatenate([pieces[(j, 0)], pieces[(j, 1)]], axis=0)
        o_ref[0, :, j * LANES:(j + 1) * LANES] = both.T.astype(BF16)


def _zero_init(t, refs):
    @pl.when(t == 0)
    def _():
        for ref in refs:
            ref[...] = jnp.zeros(ref.shape, F32)


def _dense_attn_kernel(q_ref, k_ref, v_ref, o_ref, s0, m0, s1, m1, *, plan):
    t = pl.program_id(0)
    step = functools.partial(_dense_attn_steps, q_ref, k_ref, v_ref, o_ref, plan=plan, score_keys=KEY_PIECE)
    _zero_init(t, (s1, m1))

    @pl.when(t % 2 == 0)
    def _():
        step(s0, m0, s1, m1)

    @pl.when(t % 2 == 1)
    def _():
        step(s1, m1, s0, m0)


def _attn_ab_kernel(sink_ref, qa_ref, ka_ref, va_ref, qb_ref, kb_ref, vb_ref, tab_ref, oa_ref, ob_ref,
                    sa0, ma0, sa1, ma1, sb0, mb0, sb1, mb1, *, plan, nq, tiles):
    t = pl.program_id(0)

    def step(sa_new, ma_new, sa_old, ma_old, sb_new, mb_new, sb_old, mb_old):
        _win_attn_steps(sink_ref, qb_ref, kb_ref, vb_ref, tab_ref, ob_ref, sb_new, mb_new, sb_old, mb_old,
                        nq=nq, tiles=tiles)
        _dense_attn_steps(qa_ref, ka_ref, va_ref, oa_ref, sa_new, ma_new, sa_old, ma_old, plan=plan,
                          score_keys=ka_ref.shape[1])

    _zero_init(t, (sa1, ma1, sb1, mb1))

    @pl.when(t % 2 == 0)
    def _():
        step(sa0, ma0, sa1, ma1, sb0, mb0, sb1, mb1)

    @pl.when(t % 2 == 1)
    def _():
        step(sa1, ma1, sa0, ma0, sb1, mb1, sb0, mb0)


def _pipeline_specs(b, s, q_tile):
    nq = s // q_tile
    tiles = b * nq

    def cur(t):
        return jnp.minimum(t, tiles - 1)

    def prev(t):
        return jnp.maximum(t - 1, 0)

    q_spec = lambda w: pl.BlockSpec((1, q_tile, w), lambda t: (cur(t) // nq, cur(t) % nq, 0))
    k_spec = lambda w: pl.BlockSpec((1, s, w), lambda t: (cur(t) // nq, 0, 0))
    v_spec = lambda w: pl.BlockSpec((1, s, w), lambda t: (prev(t) // nq, 0, 0))
    o_spec = lambda w: pl.BlockSpec((1, q_tile, w), lambda t: (prev(t) // nq, prev(t) % nq, 0))
    vt_spec = lambda h: pl.BlockSpec((1, h, s), lambda t: (prev(t) // nq, 0, 0))
    return nq, tiles, q_spec, k_spec, v_spec, vt_spec, o_spec


def _score_scratch_t(keys, cols):
    return [pltpu.VMEM(shape, F32) for _ in range(2) for shape in ((keys, cols), (8, cols))]


def _dense_attn(q, k, v, plan, q_tile, name):
    b, s, qw = q.shape
    ow = (max(h[2] for g in plan for h in g[0]) + 1) * LANES
    stacked_rows = sum(len(g[0]) for g in plan) * q_tile
    _, tiles, q_spec, k_spec, _, vt_spec, o_spec = _pipeline_specs(b, s, q_tile)
    return pl.pallas_call(
        functools.partial(_dense_attn_kernel, plan=plan),
        grid=(tiles + 1,),
        in_specs=[q_spec(qw), k_spec(k.shape[2]), vt_spec(v.shape[1])],
        out_specs=o_spec(ow),
        out_shape=jax.ShapeDtypeStruct((b, s, ow), BF16),
        scratch_shapes=_score_scratch_t(s, stacked_rows),
        compiler_params=_params("arbitrary"),
        name=name,
    )(q, k, v)


def _attn_ab(qa, ka, va, qb, kb, vb, tab, sink, plan):
    b, s, qw = qa.shape
    rows_a = sum(len(g[0]) for g in plan) * Q_TILE
    nq, tiles, q_spec, k_spec, v_spec, vt_spec, o_spec = _pipeline_specs(b, s, Q_TILE)
    return pl.pallas_call(
        functools.partial(_attn_ab_kernel, plan=plan, nq=nq, tiles=tiles),
        grid=(tiles + 1,),
        in_specs=[
            pl.BlockSpec(memory_space=pltpu.SMEM),
            q_spec(qw), k_spec(ka.shape[2]), vt_spec(va.shape[1]),
            q_spec(qb.shape[2]), k_spec(kb.shape[2]), v_spec(vb.shape[2]),
            _const_spec(tab.shape),
        ],
        out_specs=[o_spec(qw), o_spec(qb.shape[2])],
        out_shape=[jax.ShapeDtypeStruct((b, s, qw), BF16), jax.ShapeDtypeStruct((b, s, qb.shape[2]), BF16)],
        scratch_shapes=_score_scratch_t(s, rows_a) + _score_scratch_t(WIN_KEYS, B_HEADS * Q_TILE),
        compiler_params=_params("arbitrary"),
        name="attn_ab",
    )(sink, qa, ka, va, qb, kb, vb, tab)


def _bias_tab_kernel(rb_ref, idx_ref, o_ref):
    for p in range(WIN_PLACEMENTS):
        idx = idx_ref[p]
        for h in range(B_HEADS):
            acc = jnp.full(idx.shape, NEG_INF, F32)
            for bkt in range(NUM_BUCKETS):
                acc = jnp.where(idx == bkt, rb_ref[bkt, h] * LOG2_E, acc)
            o_ref[p, h] = acc


def _t5_bucket(rel):
    nb = NUM_BUCKETS // 2
    max_exact = nb // 2
    ret = jnp.where(rel > 0, nb, 0)
    n = jnp.abs(rel)
    large = max_exact + (jnp.log(jnp.maximum(n, 1).astype(jnp.float32) / max_exact)
                         / math.log(MAX_DISTANCE / max_exact) * (nb - max_exact)).astype(jnp.int32)
    large = jnp.minimum(large, nb - 1)
    return ret + jnp.where(n < max_exact, n, large)


def _bias_table(rel_bias):
    p = jnp.arange(WIN_PLACEMENTS)[:, None, None]
    j = jnp.arange(WIN_KEYS)[None, :, None]
    r = jnp.arange(Q_TILE)[None, None, :]
    rel = j + p * WINDOW - 2 * WINDOW - r
    idx = jnp.where(jnp.abs(rel) <= WINDOW, _t5_bucket(rel), -1).astype(jnp.int32)
    shape = (WIN_PLACEMENTS, B_HEADS, WIN_KEYS, Q_TILE)
    return pl.pallas_call(
        _bias_tab_kernel,
        in_specs=[pl.BlockSpec(memory_space=pltpu.SMEM),
                  pl.BlockSpec(idx.shape, lambda: (0, 0, 0))],
        out_specs=pl.BlockSpec(shape, lambda: (0, 0, 0, 0)),
        out_shape=jax.ShapeDtypeStruct(shape, F32),
        compiler_params=pltpu.CompilerParams(vmem_limit_bytes=V7X_VMEM_LIMIT_BYTES),
        name="win_bias_table",
    )(rel_bias, idx)


def _mix_out_kernel(x_ref, m_ref, g_ref, oa_ref, ob_ref, oc_ref, wg_ref, wa_ref, wb_ref, wc_ref,
                    wo_ref, o_ref):
    x = x_ref[0]
    d = x.shape[-1]
    h = _norm_mod(x, g_ref[...], m_ref[0, 0], m_ref[0, 1]).astype(BF16)
    gates = jnp.dot(h, wg_ref[...], preferred_element_type=F32)
    ya = jnp.dot(oa_ref[0], wa_ref[...], preferred_element_type=F32)
    yb = jnp.dot(ob_ref[0], wb_ref[...], preferred_element_type=F32)
    yc = jnp.dot(oc_ref[0], wc_ref[...], preferred_element_type=F32)
    merged = (_sigmoid(gates[:, :d]) * ya + _sigmoid(gates[:, d:2 * d]) * yb
              + _sigmoid(gates[:, 2 * d:]) * yc)
    out = jnp.dot(merged.astype(BF16), wo_ref[...], preferred_element_type=F32)
    o_ref[0] = x + m_ref[0, 2] * out


def _mix_out(x, mods, mod_idx, gain, o_a, o_b, o_c, w_gates, w_a, w_b, w_c, w_o):
    b, s, d = x.shape
    row = lambda w: pl.BlockSpec((1, ROW_TILE, w), lambda i, j: (i, j, 0))
    return pl.pallas_call(
        _mix_out_kernel,
        grid=(b, s // ROW_TILE),
        in_specs=[
            row(d),
            pl.BlockSpec((1, 3, 1, d), lambda i, j: (i, mod_idx, 0, 0)),
            _const_spec((1, d)),
            row(o_a.shape[2]), row(o_b.shape[2]), row(o_c.shape[2]),
            _const_spec(w_gates.shape), _const_spec(w_a.shape), _const_spec(w_b.shape),
            _const_spec(w_c.shape), _const_spec(w_o.shape),
        ],
        out_specs=row(d),
        out_shape=jax.ShapeDtypeStruct((b, s, d), F32),
        compiler_params=_params("arbitrary", "arbitrary"),
        name="mix_out",
    )(x, mods, gain.reshape(1, d), o_a, o_b, o_c, w_gates, w_a, w_b, w_c, w_o)


def _rope_angles(pos, dim):
    inv = ROPE_THETA ** (-jnp.arange(0, dim, 2, dtype=jnp.float32) / dim)
    ang = pos.astype(jnp.float32)[:, None] * inv[None, :]
    return jnp.cos(ang), jnp.sin(ang)


def _rope_tables(s):
    rows = s // GRID_W
    t = jnp.arange(s)
    row_pos = jnp.repeat(jnp.arange(rows), GRID_W)
    col_pos = jnp.tile(jnp.arange(GRID_W), rows)
    cr, sr = _rope_angles(row_pos, HEAD_DIM // 2)
    cc, sc = _rope_angles(col_pos, HEAD_DIM // 2)
    cos_h = jnp.concatenate([cr, cr, cc, cc], axis=-1)
    sin_h = jnp.concatenate([-sr, sr, -sc, sc], axis=-1)
    reps = LANES // HEAD_DIM
    cos_a = jnp.tile(cos_h, (1, reps))
    sin_a = jnp.tile(sin_h, (1, reps))
    cs, ss = _rope_angles(t, C_ROPE_DIM)
    ones = jnp.ones((s, C_NOPE_DIM), F32)
    zeros_lo = jnp.zeros((s, C_NOPE_DIM), F32)
    zeros_hi = jnp.zeros((s, LANES - C_QK_DIM), F32)
    t1 = jnp.concatenate([ones, cs, cs, zeros_hi], axis=-1)
    t2 = jnp.concatenate([zeros_lo, -ss, ss, zeros_hi], axis=-1)
    return cos_a, sin_a, t1, t2


def _attn_plans():
    plan_a = tuple((tuple((j, g, j, g) for j in range(Q_SLOTS)), 0, 0) for g in range(A_KV_HEADS))
    plan_c = tuple((((h, None, h // 2, h % 2),), h, h // 2) for h in range(C_HEADS))
    return plan_a, plan_c


def kernel(x, c, ada_w, ada_b, norm_ffn1, ffn1_w_gu, ffn1_w_down, norm_mix, w_in, a_q_norm, a_k_norm,
           b_sink, rel_bias, c_q_lat_norm, c_w_q_up, c_kv_lat_norm, c_w_kv_up, w_br_a, w_br_b, w_br_c,
           w_out, norm_ffn2, ffn2_w_gu, ffn2_w_down, final_norm):
    b, s, d = x.shape
    assert d == D_MODEL and s % max(Q_TILE, C_Q_TILE, ROW_TILE) == 0 and s >= WIN_KEYS and s % GRID_W == 0
    depth = ada_w.shape[0]

    mods = _ada_mods(c, ada_w, ada_b)
    tabs = _rope_tables(s)
    bias_tab = _bias_table(rel_bias)
    plan_a, plan_c = _attn_plans()

    proj_cols = _proj_columns()
    q_up_cols = _q_up_columns()
    kv_up_cols = _kv_up_columns()
    head_rows = np.asarray([h * HEAD_DIM + dd for h in _q_head_order() for dd in range(HEAD_DIM)])
    partner = np.arange(HEAD_DIM) ^ ROPE_HALF
    reps = LANES // HEAD_DIM

    for l in range(depth):
        x = _ffn(x, mods, 3 * l, norm_ffn1[l], ffn1_w_gu[l], ffn1_w_down[l])

        w_attn = _take_cols(w_in[l], proj_cols).astype(BF16)
        aqn = jnp.stack([jnp.tile(a_q_norm[l], reps), jnp.tile(a_q_norm[l][partner], reps)])
        akn = jnp.stack([jnp.tile(a_k_norm[l], reps), jnp.tile(a_k_norm[l][partner], reps)])
        wq = _take_cols(c_w_q_up[l], q_up_cols).astype(BF16)
        wkv = _take_cols(c_w_kv_up[l], kv_up_cols).astype(BF16)
        qa, ka, va, qb, kb, vb, qc, kc, vc = _proj(
            x, mods, 3 * l + 1, norm_mix[l], w_attn, aqn, akn,
            c_q_lat_norm[l].reshape(1, -1), wq, c_kv_lat_norm[l].reshape(1, -1), wkv, tabs)

        o_a, o_b = _attn_ab(qa, ka, va, qb, kb, vb, bias_tab, b_sink[l], plan_a)
        o_c = _dense_attn(qc, kc, vc, plan_c, C_Q_TILE, "attn_c")

        x = _mix_out(
            x, mods, 3 * l + 1, norm_mix[l], o_a, o_b, o_c,
            w_in[l][:, GATE_OFF:].astype(BF16),
            w_br_a[l][head_rows].astype(BF16), w_br_b[l][head_rows].astype(BF16),
            w_br_c[l].astype(BF16), w_out[l].astype(BF16))

        x = _ffn(x, mods, 3 * l + 2, norm_ffn2[l], ffn2_w_gu[l], ffn2_w_down[l],
                 final_gain=final_norm if l == depth - 1 else None)
    return x
```

```python
import functools
import math

import numpy as np
import jax
import jax.numpy as jnp
from jax import lax
from jax.experimental import pallas as pl
from jax.experimental.pallas import tpu as pltpu

D_MODEL = 1024
DEPTH = 2
HEAD_DIM = 64
A_HEADS = 6
A_KV_HEADS = 2
B_HEADS = 6
B_KV_HEADS = 2
C_HEADS = 4
C_Q_RANK = 256
C_KV_RANK = 128
C_NOPE_DIM = 64
C_ROPE_DIM = 32
C_V_DIM = 64
C_QK_DIM = C_NOPE_DIM + C_ROPE_DIM
D_FF = 2816
GRID_W = 64
WINDOW = 128
NUM_BUCKETS = 32
MAX_DISTANCE = 128
ROPE_THETA = 10000.0
ADA_CHUNKS = 9
EPS = 1e-6
NEG_INF = -1e30

A_Q_W = A_HEADS * HEAD_DIM
A_KV_W = A_KV_HEADS * HEAD_DIM
B_Q_W = B_HEADS * HEAD_DIM
B_KV_W = B_KV_HEADS * HEAD_DIM
IN_SIZES = (A_Q_W, A_KV_W, A_KV_W, B_Q_W, B_KV_W, B_KV_W,
            C_Q_RANK, C_KV_RANK, C_ROPE_DIM, D_MODEL, D_MODEL, D_MODEL)
IN_OFFS = tuple(int(v) for v in np.cumsum((0,) + IN_SIZES))
GATE_OFF = IN_OFFS[9]
ROPE_HALF = HEAD_DIM // 4
assert ROPE_HALF == C_ROPE_DIM // 2

LANES = 128
MXU_TILE = 256
V7X_VMEM_LIMIT_BYTES = 60000 * 1024

VT_ONES = 16
VT_HEAD_H = HEAD_DIM + VT_ONES
VT_SLOT_H = 2 * VT_HEAD_H
LOG2_E = math.log2(math.e)
ROW_TILE = 1024
FFN_CHUNK = MXU_TILE
Q_TILE = 256
C_Q_TILE = 512
KEY_PIECE = 256
WIN_KEYS = Q_TILE + 2 * WINDOW
WIN_PLACEMENTS = 3

BF16 = jnp.bfloat16
F32 = jnp.float32


def _params(*sem):
    return pltpu.CompilerParams(dimension_semantics=sem, vmem_limit_bytes=V7X_VMEM_LIMIT_BYTES)


def _const_spec(shape):
    nd = len(shape)
    return pl.BlockSpec(shape, lambda *_: (0,) * nd, pipeline_mode=pl.Buffered(1))


def _sigmoid(v):
    return 1.0 / (1.0 + jnp.exp(-v))


def _norm_mod(x, gain, shift, scale):
    ms = jnp.mean(x * x, axis=-1, keepdims=True)
    y = x * lax.rsqrt(ms + EPS) * gain
    return y * (1.0 + scale) + shift


def _ada_kernel(c_ref, w_ref, b_ref, o_ref):
    c = c_ref[...]
    cond = (c * _sigmoid(c)).astype(BF16)
    o_ref[0] = jnp.dot(cond, w_ref[0].astype(BF16), preferred_element_type=F32) + b_ref[0]


def _ada_mods(c, ada_w, ada_b):
    depth, d, _ = ada_w.shape
    b = c.shape[0]
    bias = ada_b.reshape(depth * ADA_CHUNKS, 1, d)
    out = pl.pallas_call(
        _ada_kernel,
        grid=(depth, ADA_CHUNKS),
        in_specs=[
            pl.BlockSpec((b, d), lambda l, j: (0, 0)),
            pl.BlockSpec((1, d, d), lambda l, j: (l, 0, j)),
            pl.BlockSpec((1, 1, d), lambda l, j: (l * ADA_CHUNKS + j, 0, 0)),
        ],
        out_specs=pl.BlockSpec((1, b, d), lambda l, j: (l * ADA_CHUNKS + j, 0, 0)),
        out_shape=jax.ShapeDtypeStruct((depth * ADA_CHUNKS, b, d), F32),
        compiler_params=_params("arbitrary", "arbitrary"),
        name="ada_mods",
    )(c, ada_w, bias)
    return jnp.transpose(out, (1, 0, 2))[:, :, None, :]


def _ffn_kernel(x_ref, m_ref, g_ref, wgu_ref, wd_ref, *rest, final):
    o_ref = rest[-1]
    x = x_ref[0]
    h = _norm_mod(x, g_ref[...], m_ref[0, 0], m_ref[0, 1]).astype(BF16)
    y = None
    for c in range(D_FF // FFN_CHUNK):
        gate = jnp.dot(h, wgu_ref[:, c * FFN_CHUNK:(c + 1) * FFN_CHUNK], preferred_element_type=F32)
        up = jnp.dot(h, wgu_ref[:, D_FF + c * FFN_CHUNK:D_FF + (c + 1) * FFN_CHUNK], preferred_element_type=F32)
        act = (gate * _sigmoid(gate) * up).astype(BF16)
        yc = jnp.dot(act, wd_ref[c * FFN_CHUNK:(c + 1) * FFN_CHUNK, :], preferred_element_type=F32)
        y = yc if c == 0 else y + yc
    out = x + (0.5 * m_ref[0, 2]) * y
    if final:
        ms = jnp.mean(out * out, axis=-1, keepdims=True)
        out = out * lax.rsqrt(ms + EPS) * rest[0][...]
    o_ref[0] = out


def _layer_spec(shape, layer):
    nd = len(shape)
    return pl.BlockSpec((None,) + tuple(shape), lambda *_: (layer,) + (0,) * nd, pipeline_mode=pl.Buffered(1))


def _ffn(x, mods, mod_idx, gain, w_gu, w_down, layer, final_gain=None):
    b, s, d = x.shape
    final = final_gain is not None
    in_specs = [
        pl.BlockSpec((1, ROW_TILE, d), lambda i, j: (i, j, 0)),
        pl.BlockSpec((1, 3, 1, d), lambda i, j: (i, mod_idx, 0, 0)),
        _const_spec((1, d)),
        _layer_spec((d, 2 * D_FF), layer),
        _layer_spec((D_FF, d), layer),
    ]
    args = [x, mods, gain.reshape(1, d), w_gu, w_down]
    if final:
        in_specs.append(_const_spec((1, d)))
        args.append(final_gain.reshape(1, d))
    return pl.pallas_call(
        functools.partial(_ffn_kernel, final=final),
        grid=(b, s // ROW_TILE),
        in_specs=in_specs,
        out_specs=pl.BlockSpec((1, ROW_TILE, d), lambda i, j: (i, j, 0)),
        out_shape=jax.ShapeDtypeStruct((b, s, d), F32),
        compiler_params=_params("arbitrary", "arbitrary"),
        name="ffn_final" if final else "ffn",
    )(*args)


_SEG_SIZES = (A_Q_W, A_KV_W, A_KV_W, B_Q_W, B_KV_W, B_KV_W, C_Q_RANK, C_KV_RANK, LANES)
_SEG_OFFS = tuple(int(v) for v in np.cumsum((0,) + _SEG_SIZES))
PROJ_W = _SEG_OFFS[-1]
Q_SLOTS = A_HEADS // A_KV_HEADS
C_Q_W = C_HEADS * LANES
C_V_W = C_HEADS * C_V_DIM


def _q_head_order():
    g = A_HEADS // A_KV_HEADS
    return [k * g + j for j in range(g) for k in range(A_KV_HEADS)]


def _proj_columns():
    aq, ak, av, bq, bk, bv, cq, ckv, ckr = IN_OFFS[:9]
    order = _q_head_order()
    cols = []
    cols += [aq + h * HEAD_DIM + d for h in order for d in range(HEAD_DIM)]
    cols += list(range(ak, ak + A_KV_W))
    cols += list(range(av, av + A_KV_W))
    cols += [bq + h * HEAD_DIM + d for h in order for d in range(HEAD_DIM)]
    cols += list(range(bk, bk + B_KV_W))
    cols += list(range(bv, bv + B_KV_W))
    cols += list(range(cq, cq + C_Q_RANK))
    cols += list(range(ckv, ckv + C_KV_RANK))
    cols += [-1] * C_NOPE_DIM + [ckr + d for d in range(C_ROPE_DIM)] + [-1] * (LANES - C_QK_DIM)
    cols = np.asarray(cols, np.int32)
    assert cols.shape[0] == PROJ_W
    return cols


def _take_cols(w, cols):
    picked = jnp.take(w, jnp.asarray(np.maximum(cols, 0)), axis=1)
    return jnp.where(jnp.asarray(cols >= 0)[None, :], picked, 0.0)


def _q_up_columns():
    rdims = np.arange(C_ROPE_DIM)
    rpartner = rdims ^ ROPE_HALF
    pad_hi = LANES - C_QK_DIM
    direct, part = [], []
    for h in range(C_HEADS):
        base = h * C_QK_DIM
        direct += [base + d for d in range(C_NOPE_DIM)] + [base + C_NOPE_DIM + d for d in rdims] + [-1] * pad_hi
        part += [-1] * C_NOPE_DIM + [base + C_NOPE_DIM + d for d in rpartner] + [-1] * pad_hi
    return np.asarray(direct + part, np.int32)


def _kv_up_columns():
    per = C_NOPE_DIM + C_V_DIM
    keys, vals = [], []
    for h in range(C_HEADS):
        keys += [h * per + d for d in range(C_NOPE_DIM)] + [-1] * (LANES - C_NOPE_DIM)
        vals += [h * per + C_NOPE_DIM + d for d in range(C_V_DIM)]
    return np.asarray(keys + vals, np.int32)


def _rope_partner(t):
    lane = lax.broadcasted_iota(jnp.int32, t.shape, 1)
    ahead = pltpu.roll(t, LANES - ROPE_HALF, 1)
    behind = pltpu.roll(t, ROPE_HALF, 1)
    return jnp.where((lane & ROPE_HALF) == 0, ahead, behind)


def _half_rms(t, n):
    lane = lax.broadcasted_iota(jnp.int32, t.shape, 1)
    left = lane < HEAD_DIM
    sq = t * t
    s_l = jnp.sum(jnp.where(left, sq, 0.0), axis=-1, keepdims=True)
    s_r = jnp.sum(jnp.where(left, 0.0, sq), axis=-1, keepdims=True)
    return jnp.where(left, lax.rsqrt(s_l / n + EPS), lax.rsqrt(s_r / n + EPS))


def _proj_kernel(x_ref, m_ref, g_ref, w_ref, aqn_ref, akn_ref, cqn_ref, wq_ref, ckvn_ref, wkv_ref,
                 ac_ref, as_ref, t1_ref, t2_ref,
                 qa_ref, ka_ref, va_ref, qb_ref, kb_ref, vb_ref, qc_ref, kc_ref, vc_ref):
    x = x_ref[0]
    h = _norm_mod(x, g_ref[...], m_ref[0, 0], m_ref[0, 1]).astype(BF16)
    p = jnp.dot(h, w_ref[...], preferred_element_type=F32)
    seg = [p[:, _SEG_OFFS[i]:_SEG_OFFS[i + 1]] for i in range(len(_SEG_SIZES))]
    aq, ak, av, bq, bk, bv, cq, ckv, kr = seg

    cos_a = ac_ref[...]
    sin_a = as_ref[...]
    qa_scale = HEAD_DIM ** -0.5 * LOG2_E
    qb_scale = HEAD_DIM ** -0.5 * LOG2_E
    qc_scale = C_QK_DIM ** -0.5 * LOG2_E
    gq_c = aqn_ref[0:1, :] * cos_a
    gq_s = aqn_ref[1:2, :] * sin_a
    for j in range(Q_SLOTS):
        sl = slice(j * LANES, (j + 1) * LANES)
        r = _half_rms(aq[:, sl], float(HEAD_DIM))
        qa_ref[0, :, sl] = ((aq[:, sl] * gq_c + _rope_partner(aq[:, sl]) * gq_s) * (r * qa_scale)).astype(BF16)
        qb_ref[0, :, sl] = (bq[:, sl] * qb_scale).astype(BF16)
    r = _half_rms(ak, float(HEAD_DIM))
    ka_ref[0] = ((ak * (akn_ref[0:1, :] * cos_a) + _rope_partner(ak) * (akn_ref[1:2, :] * sin_a)) * r).astype(BF16)
    ones_t = jnp.ones((VT_ONES, x.shape[0]), BF16)
    av_t = av.T
    for hh in range(A_KV_HEADS):
        va_ref[0, hh * VT_HEAD_H:hh * VT_HEAD_H + HEAD_DIM, :] = av_t[hh * HEAD_DIM:(hh + 1) * HEAD_DIM].astype(BF16)
        va_ref[0, hh * VT_HEAD_H + HEAD_DIM:(hh + 1) * VT_HEAD_H, :] = ones_t
    kb_ref[0] = bk.astype(BF16)
    vb_ref[0] = bv.astype(BF16)

    t1 = t1_ref[...]
    t2 = t2_ref[...]
    cqn = cq * lax.rsqrt(jnp.mean(cq * cq, axis=-1, keepdims=True) + EPS) * cqn_ref[...]
    qu = jnp.dot(cqn.astype(BF16), wq_ref[...], preferred_element_type=F32)
    ckvn = ckv * lax.rsqrt(jnp.mean(ckv * ckv, axis=-1, keepdims=True) + EPS) * ckvn_ref[...]
    kvu = jnp.dot(ckvn.astype(BF16), wkv_ref[...], preferred_element_type=F32)
    k_rope = kr * t1 + _rope_partner(kr) * t2
    for hh in range(C_HEADS):
        sl = slice(hh * LANES, (hh + 1) * LANES)
        sl2 = slice(C_Q_W + hh * LANES, C_Q_W + (hh + 1) * LANES)
        qc_ref[0, :, sl] = ((qu[:, sl] * t1 + qu[:, sl2] * t2) * qc_scale).astype(BF16)
        kc_ref[0, :, sl] = (kvu[:, sl] + k_rope).astype(BF16)
    vc_t = kvu[:, C_Q_W:].T
    for hh in range(C_HEADS):
        vc_ref[0, hh * VT_HEAD_H:hh * VT_HEAD_H + C_V_DIM, :] = vc_t[hh * C_V_DIM:(hh + 1) * C_V_DIM].astype(BF16)
        vc_ref[0, hh * VT_HEAD_H + C_V_DIM:(hh + 1) * VT_HEAD_H, :] = ones_t


def _proj(x, mods, mod_idx, gain, w_attn, aqn, akn, cqn, wq, ckvn, wkv, tabs):
    b, s, d = x.shape
    row = lambda w: pl.BlockSpec((1, ROW_TILE, w), lambda i, j: (i, j, 0))
    col = lambda h: pl.BlockSpec((1, h, ROW_TILE), lambda i, j: (i, 0, j))
    tab = pl.BlockSpec((ROW_TILE, LANES), lambda i, j: (j, 0))
    outs = (("row", A_Q_W), ("row", A_KV_W), ("col", A_KV_HEADS * VT_HEAD_H), ("row", B_Q_W), ("row", B_KV_W),
            ("row", B_KV_W), ("row", C_Q_W), ("row", C_Q_W), ("col", C_HEADS * VT_HEAD_H))
    return pl.pallas_call(
        _proj_kernel,
        grid=(b, s // ROW_TILE),
        in_specs=[
            row(d),
            pl.BlockSpec((1, 3, 1, d), lambda i, j: (i, mod_idx, 0, 0)),
            _const_spec((1, d)),
            _const_spec((d, PROJ_W)),
            _const_spec((2, LANES)),
            _const_spec((2, LANES)),
            _const_spec((1, C_Q_RANK)),
            _const_spec((C_Q_RANK, 2 * C_Q_W)),
            _const_spec((1, C_KV_RANK)),
            _const_spec((C_KV_RANK, C_Q_W + C_V_W)),
            tab, tab, tab, tab,
        ],
        out_specs=[row(w) if kind == "row" else col(w) for kind, w in outs],
        out_shape=[jax.ShapeDtypeStruct((b, s, w) if kind == "row" else (b, w, s), BF16) for kind, w in outs],
        compiler_params=_params("arbitrary", "arbitrary"),
        name="attn_proj",
    )(x, mods, gain.reshape(1, d), w_attn, aqn, akn, cqn, wq, ckvn, wkv, *tabs)


def _lane_left(shape):
    return lax.broadcasted_iota(jnp.int32, shape, 1) < HEAD_DIM


def _dense_attn_steps(q_ref, k_ref, vt_ref, o_ref, s_new, m_new, s_old, m_old, *, plan, score_keys):
    tq = q_ref.shape[1]
    n_keys = k_ref.shape[1]
    left = _lane_left((tq, LANES))
    col = 0
    pieces = {}
    for heads, k_slot, v_slot in plan:
        qs = []
        for q_slot, q_half, _, _ in heads:
            q = q_ref[0, :, q_slot * LANES:(q_slot + 1) * LANES]
            if q_half is not None:
                q = jnp.where(left if q_half == 0 else jnp.logical_not(left), q, jnp.zeros_like(q))
            qs.append(q)
        q = qs[0] if len(qs) == 1 else jnp.concatenate(qs, axis=0)
        cols = slice(col, col + q.shape[0])
        col += q.shape[0]
        v_half = heads[0][3]
        assert all(h[3] == v_half for h in heads)
        v_rows = slice(v_slot * VT_SLOT_H + v_half * VT_HEAD_H, v_slot * VT_SLOT_H + (v_half + 1) * VT_HEAD_H)
        m_prev = m_old[0:1, cols]
        m_run = o_run = None
        for c in range(n_keys // KEY_PIECE):
            if c * KEY_PIECE % score_keys == 0:
                keys = slice(c * KEY_PIECE, c * KEY_PIECE + score_keys)
                k = k_ref[0, keys, k_slot * LANES:(k_slot + 1) * LANES]
                s = lax.dot_general(k, q, (((1,), (1,)), ((), ())), preferred_element_type=F32)
                s_new[keys, cols] = s
                mx = jnp.max(s, axis=0, keepdims=True)
                m_run = mx if c == 0 else jnp.maximum(m_run, mx)

            keys = slice(c * KEY_PIECE, (c + 1) * KEY_PIECE)
            p = jnp.exp2(s_old[keys, cols] - m_prev)
            pv = jnp.dot(vt_ref[0, v_rows, keys], p.astype(BF16), preferred_element_type=F32)
            o_run = pv if c == 0 else o_run + pv
        m_new[:, cols] = jnp.broadcast_to(m_run, (m_new.shape[0], q.shape[0]))
        o = o_run[:HEAD_DIM] * (1.0 / o_run[HEAD_DIM:HEAD_DIM + 1])
        for n, (_, _, out_slot, out_half) in enumerate(heads):
            pieces[(out_slot, out_half)] = o[:, n * tq:(n + 1) * tq]
    for j in range(o_ref.shape[2] // LANES):
        both = jnp.concatenate([pieces[(j, 0)], pieces[(j, 1)]], axis=0)
        o_ref[0, :, j * LANES:(j + 1) * LANES] = both.T.astype(BF16)


def _win_start(tile, nq, s_len):
    q0 = (tile % nq) * Q_TILE
    start = jnp.clip(q0 - WINDOW, 0, s_len - WIN_KEYS)
    place = (start - q0 + 2 * WINDOW) // WINDOW
    return pl.multiple_of(start, LANES), place


def _win_attn_steps(sink_ref, q_ref, k_ref, v_ref, tab_ref, o_ref, s_new, m_new, s_old, m_old, *,
                    nq, tiles):
    t = pl.program_id(0)
    s_len = k_ref.shape[1]
    start_new, place = _win_start(jnp.minimum(t, tiles - 1), nq, s_len)
    start_old, _ = _win_start(jnp.maximum(t - 1, 0), nq, s_len)
    kw = k_ref[0, pl.ds(start_new, WIN_KEYS), :]
    vw_t = v_ref[0, pl.ds(start_old, WIN_KEYS), :].astype(F32).T.astype(BF16)
    ones_t = jnp.ones((VT_ONES, WIN_KEYS), BF16)
    left = _lane_left((Q_TILE, LANES))
    g = B_HEADS // B_KV_HEADS
    slots = [q_ref[0, :, j * LANES:(j + 1) * LANES] for j in range(Q_SLOTS)]
    pieces = {}
    for side in range(B_KV_HEADS):
        keep = left if side == 0 else jnp.logical_not(left)
        q = jnp.concatenate([jnp.where(keep, qs, jnp.zeros_like(qs)) for qs in slots], axis=0)
        s = lax.dot_general(kw, q, (((1,), (1,)), ((), ())), preferred_element_type=F32)
        cols = slice(side * g * Q_TILE, (side + 1) * g * Q_TILE)
        sinks = [sink_ref[side * g + j] * LOG2_E for j in range(g)]
        sb, ms = [], []
        for j in range(g):
            sj = s[:, j * Q_TILE:(j + 1) * Q_TILE] + tab_ref[place, side * g + j]
            sb.append(sj)
            ms.append(jnp.maximum(jnp.max(sj, axis=0, keepdims=True), sinks[j]))
        s_new[:, cols] = jnp.concatenate(sb, axis=1)
        m_new[:, cols] = jnp.broadcast_to(jnp.concatenate(ms, axis=1), (m_new.shape[0], g * Q_TILE))

        m_prev = m_old[0:1, cols]
        p = jnp.exp2(s_old[:, cols] - m_prev)
        vt = jnp.concatenate([vw_t[side * HEAD_DIM:(side + 1) * HEAD_DIM], ones_t], axis=0)
        pv = jnp.dot(vt, p.astype(BF16), preferred_element_type=F32)
        for j in range(g):
            hc = slice(j * Q_TILE, (j + 1) * Q_TILE)
            l = pv[HEAD_DIM:HEAD_DIM + 1, hc] + jnp.exp2(sinks[j] - m_prev[:, hc])
            pieces[(j, side)] = pv[:HEAD_DIM, hc] * (1.0 / l)
    for j in range(Q_SLOTS):
        both = jnp.concatenate([pieces[(j, 0)], pieces[(j, 1)]], axis=0)
        o_ref[0, :, j * LANES:(j + 1) * LANES] = both.T.astype(BF16)


def _zero_init(t, refs):
    @pl.when(t == 0)
    def _():
        for ref in refs:
            ref[...] = jnp.zeros(ref.shape, F32)


def _dense_attn_kernel(q_ref, k_ref, v_ref, o_ref, s0, m0, s1, m1, *, plan):
    t = pl.program_id(0)
    step = functools.partial(_dense_attn_steps, q_ref, k_ref, v_ref, o_ref, plan=plan, score_keys=KEY_PIECE)
    _zero_init(t, (s1, m1))

    @pl.when(t % 2 == 0)
    def _():
        step(s0, m0, s1, m1)

    @pl.when(t % 2 == 1)
    def _():
        step(s1, m1, s0, m0)


def _attn_ab_kernel(sink_ref, qa_ref, ka_ref, va_ref, qb_ref, kb_ref, vb_ref, tab_ref, oa_ref, ob_ref,
                    sa0, ma0, sa1, ma1, sb0, mb0, sb1, mb1, *, plan, nq, tiles):
    t = pl.program_id(0)

    def step(sa_new, ma_new, sa_old, ma_old, sb_new, mb_new, sb_old, mb_old):
        _win_attn_steps(sink_ref, qb_ref, kb_ref, vb_ref, tab_ref, ob_ref, sb_new, mb_new, sb_old, mb_old,
                        nq=nq, tiles=tiles)
        _dense_attn_steps(qa_ref, ka_ref, va_ref, oa_ref, sa_new, ma_new, sa_old, ma_old, plan=plan,
                          score_keys=ka_ref.shape[1])

    _zero_init(t, (sa1, ma1, sb1, mb1))

    @pl.when(t % 2 == 0)
    def _():
        step(sa0, ma0, sa1, ma1, sb0, mb0, sb1, mb1)

    @pl.when(t % 2 == 1)
    def _():
        step(sa1, ma1, sa0, ma0, sb1, mb1, sb0, mb0)


def _pipeline_specs(b, s, q_tile):
    nq = s // q_tile
    tiles = b * nq

    def cur(t):
        return jnp.minimum(t, tiles - 1)

    def prev(t):
        return jnp.maximum(t - 1, 0)

    q_spec = lambda w: pl.BlockSpec((1, q_tile, w), lambda t: (cur(t) // nq, cur(t) % nq, 0))
    k_spec = lambda w: pl.BlockSpec((1, s, w), lambda t: (cur(t) // nq, 0, 0))
    v_spec = lambda w: pl.BlockSpec((1, s, w), lambda t: (prev(t) // nq, 0, 0))
    o_spec = lambda w: pl.BlockSpec((1, q_tile, w), lambda t: (prev(t) // nq, prev(t) % nq, 0))
    vt_spec = lambda h: pl.BlockSpec((1, h, s), lambda t: (prev(t) // nq, 0, 0))
    return nq, tiles, q_spec, k_spec, v_spec, vt_spec, o_spec


def _score_scratch_t(keys, cols):
    return [pltpu.VMEM(shape, F32) for _ in range(2) for shape in ((keys, cols), (8, cols))]


def _dense_attn(q, k, v, plan, q_tile, name):
    b, s, qw = q.shape
    ow = (max(h[2] for g in plan for h in g[0]) + 1) * LANES
    stacked_rows = sum(len(g[0]) for g in plan) * q_tile
    _, tiles, q_spec, k_spec, _, vt_spec, o_spec = _pipeline_specs(b, s, q_tile)
    return pl.pallas_call(
        functools.partial(_dense_attn_kernel, plan=plan),
        grid=(tiles + 1,),
        in_specs=[q_spec(qw), k_spec(k.shape[2]), vt_spec(v.shape[1])],
        out_specs=o_spec(ow),
        out_shape=jax.ShapeDtypeStruct((b, s, ow), BF16),
        scratch_shapes=_score_scratch_t(s, stacked_rows),
        compiler_params=_params("arbitrary"),
        name=name,
    )(q, k, v)


def _attn_ab(qa, ka, va, qb, kb, vb, tab, sink, plan):
    b, s, qw = qa.shape
    rows_a = sum(len(g[0]) for g in plan) * Q_TILE
    nq, tiles, q_spec, k_spec, v_spec, vt_spec, o_spec = _pipeline_specs(b, s, Q_TILE)
    return pl.pallas_call(
        functools.partial(_attn_ab_kernel, plan=plan, nq=nq, tiles=tiles),
        grid=(tiles + 1,),
        in_specs=[
            pl.BlockSpec(memory_space=pltpu.SMEM),
            q_spec(qw), k_spec(ka.shape[2]), vt_spec(va.shape[1]),
            q_spec(qb.shape[2]), k_spec(kb.shape[2]), v_spec(vb.shape[2]),
            _const_spec(tab.shape),
        ],
        out_specs=[o_spec(qw), o_spec(qb.shape[2])],
        out_shape=[jax.ShapeDtypeStruct((b, s, qw), BF16), jax.ShapeDtypeStruct((b, s, qb.shape[2]), BF16)],
        scratch_shapes=_score_scratch_t(s, rows_a) + _score_scratch_t(WIN_KEYS, B_HEADS * Q_TILE),
        compiler_params=_params("arbitrary"),
        name="attn_ab",
    )(sink, qa, ka, va, qb, kb, vb, tab)


def _bias_tab_kernel(rb_ref, idx_ref, o_ref):
    for p in range(WIN_PLACEMENTS):
        idx = idx_ref[p]
        for h in range(B_HEADS):
            acc = jnp.full(idx.shape, NEG_INF, F32)
            for bkt in range(NUM_BUCKETS):
                acc = jnp.where(idx == bkt, rb_ref[bkt, h] * LOG2_E, acc)
            o_ref[p, h] = acc


def _t5_bucket(rel):
    nb = NUM_BUCKETS // 2
    max_exact = nb // 2
    ret = jnp.where(rel > 0, nb, 0)
    n = jnp.abs(rel)
    large = max_exact + (jnp.log(jnp.maximum(n, 1).astype(jnp.float32) / max_exact)
                         / math.log(MAX_DISTANCE / max_exact) * (nb - max_exact)).astype(jnp.int32)
    large = jnp.minimum(large, nb - 1)
    return ret + jnp.where(n < max_exact, n, large)


def _bias_table(rel_bias):
    p = jnp.arange(WIN_PLACEMENTS)[:, None, None]
    j = jnp.arange(WIN_KEYS)[None, :, None]
    r = jnp.arange(Q_TILE)[None, None, :]
    rel = j + p * WINDOW - 2 * WINDOW - r
    idx = jnp.where(jnp.abs(rel) <= WINDOW, _t5_bucket(rel), -1).astype(jnp.int32)
    shape = (WIN_PLACEMENTS, B_HEADS, WIN_KEYS, Q_TILE)
    return pl.pallas_call(
        _bias_tab_kernel,
        in_specs=[pl.BlockSpec(memory_space=pltpu.SMEM),
                  pl.BlockSpec(idx.shape, lambda: (0, 0, 0))],
        out_specs=pl.BlockSpec(shape, lambda: (0, 0, 0, 0)),
        out_shape=jax.ShapeDtypeStruct(shape, F32),
        compiler_params=pltpu.CompilerParams(vmem_limit_bytes=V7X_VMEM_LIMIT_BYTES),
        name="win_bias_table",
    )(rel_bias, idx)


def _mix_out_kernel(x_ref, m_ref, g_ref, oa_ref, ob_ref, oc_ref, wg_ref, wa_ref, wb_ref, wc_ref,
                    wo_ref, o_ref):
    x = x_ref[0]
    d = x.shape[-1]
    h = _norm_mod(x, g_ref[...], m_ref[0, 0], m_ref[0, 1]).astype(BF16)
    gates = jnp.dot(h, wg_ref[...], preferred_element_type=F32)
    ya = jnp.dot(oa_ref[0], wa_ref[...], preferred_element_type=F32)
    yb = jnp.dot(ob_ref[0], wb_ref[...], preferred_element_type=F32)
    yc = jnp.dot(oc_ref[0], wc_ref[...], preferred_element_type=F32)
    merged = (_sigmoid(gates[:, :d]) * ya + _sigmoid(gates[:, d:2 * d]) * yb
              + _sigmoid(gates[:, 2 * d:]) * yc)
    out = jnp.dot(merged.astype(BF16), wo_ref[...], preferred_element_type=F32)
    o_ref[0] = x + m_ref[0, 2] * out


def _mix_out(x, mods, mod_idx, gain, o_a, o_b, o_c, w_gates, w_a, w_b, w_c, w_o):
    b, s, d = x.shape
    row = lambda w: pl.BlockSpec((1, ROW_TILE, w), lambda i, j: (i, j, 0))
    return pl.pallas_call(
        _mix_out_kernel,
        grid=(b, s // ROW_TILE),
        in_specs=[
            row(d),
            pl.BlockSpec((1, 3, 1, d), lambda i, j: (i, mod_idx, 0, 0)),
            _const_spec((1, d)),
            row(o_a.shape[2]), row(o_b.shape[2]), row(o_c.shape[2]),
            _const_spec(w_gates.shape), _const_spec(w_a.shape), _const_spec(w_b.shape),
            _const_spec(w_c.shape), _const_spec(w_o.shape),
        ],
        out_specs=row(d),
        out_shape=jax.ShapeDtypeStruct((b, s, d), F32),
        compiler_params=_params("arbitrary", "arbitrary"),
        name="mix_out",
    )(x, mods, gain.reshape(1, d), o_a, o_b, o_c, w_gates, w_a, w_b, w_c, w_o)


def _rope_angles(pos, dim):
    inv = ROPE_THETA ** (-jnp.arange(0, dim, 2, dtype=jnp.float32) / dim)
    ang = pos.astype(jnp.float32)[:, None] * inv[None, :]
    return jnp.cos(ang), jnp.sin(ang)


def _rope_tables(s):
    rows = s // GRID_W
    t = jnp.arange(s)
    row_pos = jnp.repeat(jnp.arange(rows), GRID_W)
    col_pos = jnp.tile(jnp.arange(GRID_W), rows)
    cr, sr = _rope_angles(row_pos, HEAD_DIM // 2)
    cc, sc = _rope_angles(col_pos, HEAD_DIM // 2)
    cos_h = jnp.concatenate([cr, cr, cc, cc], axis=-1)
    sin_h = jnp.concatenate([-sr, sr, -sc, sc], axis=-1)
    reps = LANES // HEAD_DIM
    cos_a = jnp.tile(cos_h, (1, reps))
    sin_a = jnp.tile(sin_h, (1, reps))
    cs, ss = _rope_angles(t, C_ROPE_DIM)
    ones = jnp.ones((s, C_NOPE_DIM), F32)
    zeros_lo = jnp.zeros((s, C_NOPE_DIM), F32)
    zeros_hi = jnp.zeros((s, LANES - C_QK_DIM), F32)
    t1 = jnp.concatenate([ones, cs, cs, zeros_hi], axis=-1)
    t2 = jnp.concatenate([zeros_lo, -ss, ss, zeros_hi], axis=-1)
    return cos_a, sin_a, t1, t2


def _attn_plans():
    plan_a = tuple((tuple((j, g, j, g) for j in range(Q_SLOTS)), 0, 0) for g in range(A_KV_HEADS))
    plan_c = tuple((((h, None, h // 2, h % 2),), h, h // 2) for h in range(C_HEADS))
    return plan_a, plan_c


def kernel(x, c, ada_w, ada_b, norm_ffn1, ffn1_w_gu, ffn1_w_down, norm_mix, w_in, a_q_norm, a_k_norm,
           b_sink, rel_bias, c_q_lat_norm, c_w_q_up, c_kv_lat_norm, c_w_kv_up, w_br_a, w_br_b, w_br_c,
           w_out, norm_ffn2, ffn2_w_gu, ffn2_w_down, final_norm):
    b, s, d = x.shape
    assert d == D_MODEL and s % max(Q_TILE, C_Q_TILE, ROW_TILE) == 0 and s >= WIN_KEYS and s % GRID_W == 0
    depth = ada_w.shape[0]

    mods = _ada_mods(c, ada_w, ada_b)
    tabs = _rope_tables(s)
    bias_tab = _bias_table(rel_bias)
    plan_a, plan_c = _attn_plans()

    proj_cols = _proj_columns()
    q_up_cols = _q_up_columns()
    kv_up_cols = _kv_up_columns()
    head_rows = np.asarray([h * HEAD_DIM + dd for h in _q_head_order() for dd in range(HEAD_DIM)])
    partner = np.arange(HEAD_DIM) ^ ROPE_HALF
    reps = LANES // HEAD_DIM

    w_gu1, w_down1 = ffn1_w_gu.astype(BF16), ffn1_w_down.astype(BF16)
    w_gu2, w_down2 = ffn2_w_gu.astype(BF16), ffn2_w_down.astype(BF16)
    for l in range(depth):
        x = _ffn(x, mods, 3 * l, norm_ffn1[l], w_gu1, w_down1, l)

        w_attn = _take_cols(w_in[l], proj_cols).astype(BF16)
        aqn = jnp.stack([jnp.tile(a_q_norm[l], reps), jnp.tile(a_q_norm[l][partner], reps)])
        akn = jnp.stack([jnp.tile(a_k_norm[l], reps), jnp.tile(a_k_norm[l][partner], reps)])
        wq = _take_cols(c_w_q_up[l], q_up_cols).astype(BF16)
        wkv = _take_cols(c_w_kv_up[l], kv_up_cols).astype(BF16)
        qa, ka, va, qb, kb, vb, qc, kc, vc = _proj(
            x, mods, 3 * l + 1, norm_mix[l], w_attn, aqn, akn,
            c_q_lat_norm[l].reshape(1, -1), wq, c_kv_lat_norm[l].reshape(1, -1), wkv, tabs)

        o_a, o_b = _attn_ab(qa, ka, va, qb, kb, vb, bias_tab, b_sink[l], plan_a)
        o_c = _dense_attn(qc, kc, vc, plan_c, C_Q_TILE, "attn_c")

        x = _mix_out(
            x, mods, 3 * l + 1, norm_mix[l], o_a, o_b, o_c,
            w_in[l][:, GATE_OFF:].astype(BF16),
            w_br_a[l][head_rows].astype(BF16), w_br_b[l][head_rows].astype(BF16),
            w_br_c[l].astype(BF16), w_out[l].astype(BF16))

        x = _ffn(x, mods, 3 * l + 2, norm_ffn2[l], w_gu2, w_down2, l,
                 final_gain=final_norm if l == depth - 1 else None)
    return x
```

```python
import functools
import math

import numpy as np
import jax
import jax.numpy as jnp
from jax import lax
from jax.experimental import pallas as pl
from jax.experimental.pallas import tpu as pltpu

D_MODEL = 1024
DEPTH = 2
HEAD_DIM = 64
A_HEADS = 6
A_KV_HEADS = 2
B_HEADS = 6
B_KV_HEADS = 2
C_HEADS = 4
C_Q_RANK = 256
C_KV_RANK = 128
C_NOPE_DIM = 64
C_ROPE_DIM = 32
C_V_DIM = 64
C_QK_DIM = C_NOPE_DIM + C_ROPE_DIM
D_FF = 2816
GRID_W = 64
WINDOW = 128
NUM_BUCKETS = 32
MAX_DISTANCE = 128
ROPE_THETA = 10000.0
ADA_CHUNKS = 9
EPS = 1e-6
NEG_INF = -1e30

A_Q_W = A_HEADS * HEAD_DIM
A_KV_W = A_KV_HEADS * HEAD_DIM
B_Q_W = B_HEADS * HEAD_DIM
B_KV_W = B_KV_HEADS * HEAD_DIM
IN_SIZES = (A_Q_W, A_KV_W, A_KV_W, B_Q_W, B_KV_W, B_KV_W,
            C_Q_RANK, C_KV_RANK, C_ROPE_DIM, D_MODEL, D_MODEL, D_MODEL)
IN_OFFS = tuple(int(v) for v in np.cumsum((0,) + IN_SIZES))
GATE_OFF = IN_OFFS[9]
ROPE_HALF = HEAD_DIM // 4
assert ROPE_HALF == C_ROPE_DIM // 2

LANES = 128
MXU_TILE = 256
V7X_VMEM_LIMIT_BYTES = 60000 * 1024

VT_ONES = 16
VT_HEAD_H = HEAD_DIM + VT_ONES
VT_SLOT_H = 2 * VT_HEAD_H
LOG2_E = math.log2(math.e)
ROW_TILE = 1024
FFN_CHUNK = MXU_TILE
Q_TILE = 256
C_Q_TILE = 512
KEY_PIECE = 256
WIN_KEYS = Q_TILE + 2 * WINDOW
WIN_PLACEMENTS = 3

BF16 = jnp.bfloat16
F32 = jnp.float32


def _params(*sem):
    return pltpu.CompilerParams(dimension_semantics=sem, vmem_limit_bytes=V7X_VMEM_LIMIT_BYTES)


def _const_spec(shape):
    nd = len(shape)
    return pl.BlockSpec(shape, lambda *_: (0,) * nd, pipeline_mode=pl.Buffered(1))


def _sigmoid(v):
    return 1.0 / (1.0 + jnp.exp(-v))


def _norm_mod(x, gain, shift, scale):
    ms = jnp.mean(x * x, axis=-1, keepdims=True)
    y = x * lax.rsqrt(ms + EPS) * gain
    return y * (1.0 + scale) + shift


def _ada_kernel(c_ref, w_ref, b_ref, o_ref):
    c = c_ref[...]
    cond = (c * _sigmoid(c)).astype(BF16)
    o_ref[0] = jnp.dot(cond, w_ref[0].astype(BF16), preferred_element_type=F32) + b_ref[0]


def _ada_mods(c, ada_w, ada_b):
    depth, d, _ = ada_w.shape
    b = c.shape[0]
    bias = ada_b.reshape(depth * ADA_CHUNKS, 1, d)
    out = pl.pallas_call(
        _ada_kernel,
        grid=(depth, ADA_CHUNKS),
        in_specs=[
            pl.BlockSpec((b, d), lambda l, j: (0, 0)),
            pl.BlockSpec((1, d, d), lambda l, j: (l, 0, j)),
            pl.BlockSpec((1, 1, d), lambda l, j: (l * ADA_CHUNKS + j, 0, 0)),
        ],
        out_specs=pl.BlockSpec((1, b, d), lambda l, j: (l * ADA_CHUNKS + j, 0, 0)),
        out_shape=jax.ShapeDtypeStruct((depth * ADA_CHUNKS, b, d), F32),
        compiler_params=_params("arbitrary", "arbitrary"),
        name="ada_mods",
    )(c, ada_w, bias)
    return jnp.transpose(out, (1, 0, 2))[:, :, None, :]


def _ffn_kernel(x_ref, m_ref, g_ref, wgu_ref, wd_ref, *rest, final):
    o_ref = rest[-1]
    x = x_ref[0]
    h = _norm_mod(x, g_ref[...], m_ref[0, 0], m_ref[0, 1]).astype(BF16)
    y = None
    for c in range(D_FF // FFN_CHUNK):
        gate = jnp.dot(h, wgu_ref[:, c * FFN_CHUNK:(c + 1) * FFN_CHUNK], preferred_element_type=F32)
        up = jnp.dot(h, wgu_ref[:, D_FF + c * FFN_CHUNK:D_FF + (c + 1) * FFN_CHUNK], preferred_element_type=F32)
        act = (gate * _sigmoid(gate) * up).astype(BF16)
        yc = jnp.dot(act, wd_ref[c * FFN_CHUNK:(c + 1) * FFN_CHUNK, :], preferred_element_type=F32)
        y = yc if c == 0 else y + yc
    out = x + (0.5 * m_ref[0, 2]) * y
    if final:
        ms = jnp.mean(out * out, axis=-1, keepdims=True)
        out = out * lax.rsqrt(ms + EPS) * rest[0][...]
    o_ref[0] = out


def _layer_spec(shape, layer):
    nd = len(shape)
    return pl.BlockSpec((None,) + tuple(shape), lambda *_: (layer,) + (0,) * nd, pipeline_mode=pl.Buffered(1))


def _ffn(x, mods, mod_idx, gain, w_gu, w_down, layer, final_gain=None):
    b, s, d = x.shape
    final = final_gain is not None
    in_specs = [
        pl.BlockSpec((1, ROW_TILE, d), lambda i, j: (i, j, 0)),
        pl.BlockSpec((1, 3, 1, d), lambda i, j: (i, mod_idx, 0, 0)),
        _const_spec((1, d)),
        _layer_spec((d, 2 * D_FF), layer),
        _layer_spec((D_FF, d), layer),
    ]
    args = [x, mods, gain.reshape(1, d), w_gu, w_down]
    if final:
        in_specs.append(_const_spec((1, d)))
        args.append(final_gain.reshape(1, d))
    return pl.pallas_call(
        functools.partial(_ffn_kernel, final=final),
        grid=(b, s // ROW_TILE),
        in_specs=in_specs,
        out_specs=pl.BlockSpec((1, ROW_TILE, d), lambda i, j: (i, j, 0)),
        out_shape=jax.ShapeDtypeStruct((b, s, d), F32),
        compiler_params=_params("arbitrary", "arbitrary"),
        name="ffn_final" if final else "ffn",
    )(*args)


_SEG_SIZES = (A_Q_W, A_KV_W, A_KV_W, B_Q_W, B_KV_W, B_KV_W, C_Q_RANK, C_KV_RANK, LANES)
_SEG_OFFS = tuple(int(v) for v in np.cumsum((0,) + _SEG_SIZES))
PROJ_W = _SEG_OFFS[-1]
Q_SLOTS = A_HEADS // A_KV_HEADS
C_Q_W = C_HEADS * LANES
C_V_W = C_HEADS * C_V_DIM


def _q_head_order():
    g = A_HEADS // A_KV_HEADS
    return [k * g + j for j in range(g) for k in range(A_KV_HEADS)]


def _proj_columns():
    aq, ak, av, bq, bk, bv, cq, ckv, ckr = IN_OFFS[:9]
    order = _q_head_order()
    cols = []
    cols += [aq + h * HEAD_DIM + d for h in order for d in range(HEAD_DIM)]
    cols += list(range(ak, ak + A_KV_W))
    cols += list(range(av, av + A_KV_W))
    cols += [bq + h * HEAD_DIM + d for h in order for d in range(HEAD_DIM)]
    cols += list(range(bk, bk + B_KV_W))
    cols += list(range(bv, bv + B_KV_W))
    cols += list(range(cq, cq + C_Q_RANK))
    cols += list(range(ckv, ckv + C_KV_RANK))
    cols += [-1] * C_NOPE_DIM + [ckr + d for d in range(C_ROPE_DIM)] + [-1] * (LANES - C_QK_DIM)
    cols = np.asarray(cols, np.int32)
    assert cols.shape[0] == PROJ_W
    return cols


def _take_cols(w, cols):
    picked = jnp.take(w, jnp.asarray(np.maximum(cols, 0)), axis=1)
    return jnp.where(jnp.asarray(cols >= 0)[None, :], picked, 0.0)


def _q_up_columns():
    rdims = np.arange(C_ROPE_DIM)
    rpartner = rdims ^ ROPE_HALF
    pad_hi = LANES - C_QK_DIM
    direct, part = [], []
    for h in range(C_HEADS):
        base = h * C_QK_DIM
        direct += [base + d for d in range(C_NOPE_DIM)] + [base + C_NOPE_DIM + d for d in rdims] + [-1] * pad_hi
        part += [-1] * C_NOPE_DIM + [base + C_NOPE_DIM + d for d in rpartner] + [-1] * pad_hi
    return np.asarray(direct + part, np.int32)


def _kv_up_columns():
    per = C_NOPE_DIM + C_V_DIM
    keys, vals = [], []
    for h in range(C_HEADS):
        keys += [h * per + d for d in range(C_NOPE_DIM)] + [-1] * (LANES - C_NOPE_DIM)
        vals += [h * per + C_NOPE_DIM + d for d in range(C_V_DIM)]
    return np.asarray(keys + vals, np.int32)


def _rope_partner(t):
    lane = lax.broadcasted_iota(jnp.int32, t.shape, 1)
    ahead = pltpu.roll(t, LANES - ROPE_HALF, 1)
    behind = pltpu.roll(t, ROPE_HALF, 1)
    return jnp.where((lane & ROPE_HALF) == 0, ahead, behind)


def _half_rms(t, n):
    lane = lax.broadcasted_iota(jnp.int32, t.shape, 1)
    left = lane < HEAD_DIM
    sq = t * t
    s_l = jnp.sum(jnp.where(left, sq, 0.0), axis=-1, keepdims=True)
    s_r = jnp.sum(jnp.where(left, 0.0, sq), axis=-1, keepdims=True)
    return jnp.where(left, lax.rsqrt(s_l / n + EPS), lax.rsqrt(s_r / n + EPS))


def _proj_kernel(x_ref, m_ref, g_ref, w_ref, aqn_ref, akn_ref, cqn_ref, wq_ref, ckvn_ref, wkv_ref,
                 ac_ref, as_ref, t1_ref, t2_ref,
                 qa_ref, ka_ref, va_ref, qb_ref, kb_ref, vb_ref, qc_ref, kc_ref, vc_ref):
    x = x_ref[0]
    h = _norm_mod(x, g_ref[...], m_ref[0, 0], m_ref[0, 1]).astype(BF16)
    p = jnp.dot(h, w_ref[...], preferred_element_type=F32)
    seg = [p[:, _SEG_OFFS[i]:_SEG_OFFS[i + 1]] for i in range(len(_SEG_SIZES))]
    aq, ak, av, bq, bk, bv, cq, ckv, kr = seg

    cos_a = ac_ref[...]
    sin_a = as_ref[...]
    qa_scale = HEAD_DIM ** -0.5 * LOG2_E
    qb_scale = HEAD_DIM ** -0.5 * LOG2_E
    qc_scale = C_QK_DIM ** -0.5 * LOG2_E
    gq_c = aqn_ref[0:1, :] * cos_a
    gq_s = aqn_ref[1:2, :] * sin_a
    for j in range(Q_SLOTS):
        sl = slice(j * LANES, (j + 1) * LANES)
        r = _half_rms(aq[:, sl], float(HEAD_DIM))
        qa_ref[0, :, sl] = ((aq[:, sl] * gq_c + _rope_partner(aq[:, sl]) * gq_s) * (r * qa_scale)).astype(BF16)
        qb_ref[0, :, sl] = (bq[:, sl] * qb_scale).astype(BF16)
    r = _half_rms(ak, float(HEAD_DIM))
    ka_ref[0] = ((ak * (akn_ref[0:1, :] * cos_a) + _rope_partner(ak) * (akn_ref[1:2, :] * sin_a)) * r).astype(BF16)
    ones_t = jnp.ones((VT_ONES, x.shape[0]), BF16)
    av_t = av.T
    for hh in range(A_KV_HEADS):
        va_ref[0, hh * VT_HEAD_H:hh * VT_HEAD_H + HEAD_DIM, :] = av_t[hh * HEAD_DIM:(hh + 1) * HEAD_DIM].astype(BF16)
        va_ref[0, hh * VT_HEAD_H + HEAD_DIM:(hh + 1) * VT_HEAD_H, :] = ones_t
    kb_ref[0] = bk.astype(BF16)
    vb_ref[0] = bv.astype(BF16)

    t1 = t1_ref[...]
    t2 = t2_ref[...]
    cqn = cq * lax.rsqrt(jnp.mean(cq * cq, axis=-1, keepdims=True) + EPS) * cqn_ref[...]
    qu = jnp.dot(cqn.astype(BF16), wq_ref[...], preferred_element_type=F32)
    ckvn = ckv * lax.rsqrt(jnp.mean(ckv * ckv, axis=-1, keepdims=True) + EPS) * ckvn_ref[...]
    kvu = jnp.dot(ckvn.astype(BF16), wkv_ref[...], preferred_element_type=F32)
    k_rope = kr * t1 + _rope_partner(kr) * t2
    for hh in range(C_HEADS):
        sl = slice(hh * LANES, (hh + 1) * LANES)
        sl2 = slice(C_Q_W + hh * LANES, C_Q_W + (hh + 1) * LANES)
        qc_ref[0, :, sl] = ((qu[:, sl] * t1 + qu[:, sl2] * t2) * qc_scale).astype(BF16)
        kc_ref[0, :, sl] = (kvu[:, sl] + k_rope).astype(BF16)
    vc_t = kvu[:, C_Q_W:].T
    for hh in range(C_HEADS):
        vc_ref[0, hh * VT_HEAD_H:hh * VT_HEAD_H + C_V_DIM, :] = vc_t[hh * C_V_DIM:(hh + 1) * C_V_DIM].astype(BF16)
        vc_ref[0, hh * VT_HEAD_H + C_V_DIM:(hh + 1) * VT_HEAD_H, :] = ones_t


def _proj(x, mods, mod_idx, gain, w_attn, aqn, akn, cqn, wq, ckvn, wkv, tabs):
    b, s, d = x.shape
    row = lambda w: pl.BlockSpec((1, ROW_TILE, w), lambda i, j: (i, j, 0))
    col = lambda h: pl.BlockSpec((1, h, ROW_TILE), lambda i, j: (i, 0, j))
    tab = pl.BlockSpec((ROW_TILE, LANES), lambda i, j: (j, 0))
    outs = (("row", A_Q_W), ("row", A_KV_W), ("col", A_KV_HEADS * VT_HEAD_H), ("row", B_Q_W), ("row", B_KV_W),
            ("row", B_KV_W), ("row", C_Q_W), ("row", C_Q_W), ("col", C_HEADS * VT_HEAD_H))
    return pl.pallas_call(
        _proj_kernel,
        grid=(b, s // ROW_TILE),
        in_specs=[
            row(d),
            pl.BlockSpec((1, 3, 1, d), lambda i, j: (i, mod_idx, 0, 0)),
            _const_spec((1, d)),
            _const_spec((d, PROJ_W)),
            _const_spec((2, LANES)),
            _const_spec((2, LANES)),
            _const_spec((1, C_Q_RANK)),
            _const_spec((C_Q_RANK, 2 * C_Q_W)),
            _const_spec((1, C_KV_RANK)),
            _const_spec((C_KV_RANK, C_Q_W + C_V_W)),
            tab, tab, tab, tab,
        ],
        out_specs=[row(w) if kind == "row" else col(w) for kind, w in outs],
        out_shape=[jax.ShapeDtypeStruct((b, s, w) if kind == "row" else (b, w, s), BF16) for kind, w in outs],
        compiler_params=_params("arbitrary", "arbitrary"),
        name="attn_proj",
    )(x, mods, gain.reshape(1, d), w_attn, aqn, akn, cqn, wq, ckvn, wkv, *tabs)


def _lane_left(shape):
    return lax.broadcasted_iota(jnp.int32, shape, 1) < HEAD_DIM


def _dense_attn_steps(q_ref, k_ref, vt_ref, o_ref, s_new, m_new, s_old, m_old, *, plan, score_keys):
    tq = q_ref.shape[1]
    n_keys = k_ref.shape[1]
    left = _lane_left((tq, LANES))
    col = 0
    pieces = {}
    for heads, k_slot, v_slot in plan:
        qs = []
        for q_slot, q_half, _, _ in heads:
            q = q_ref[0, :, q_slot * LANES:(q_slot + 1) * LANES]
            if q_half is not None:
                q = jnp.where(left if q_half == 0 else jnp.logical_not(left), q, jnp.zeros_like(q))
            qs.append(q)
        q = qs[0] if len(qs) == 1 else jnp.concatenate(qs, axis=0)
        cols = slice(col, col + q.shape[0])
        col += q.shape[0]
        v_half = heads[0][3]
        assert all(h[3] == v_half for h in heads)
        v_rows = slice(v_slot * VT_SLOT_H + v_half * VT_HEAD_H, v_slot * VT_SLOT_H + (v_half + 1) * VT_HEAD_H)
        m_prev = m_old[0:1, cols]
        m_run = o_run = None
        for c in range(n_keys // KEY_PIECE):
            if c * KEY_PIECE % score_keys == 0:
                keys = slice(c * KEY_PIECE, c * KEY_PIECE + score_keys)
                k = k_ref[0, keys, k_slot * LANES:(k_slot + 1) * LANES]
                s = lax.dot_general(k, q, (((1,), (1,)), ((), ())), preferred_element_type=F32)
                s_new[keys, cols] = s
                mx = jnp.max(s, axis=0, keepdims=True)
                m_run = mx if c == 0 else jnp.maximum(m_run, mx)

            keys = slice(c * KEY_PIECE, (c + 1) * KEY_PIECE)
            p = jnp.exp2(s_old[keys, cols] - m_prev)
            pv = jnp.dot(vt_ref[0, v_rows, keys], p.astype(BF16), preferred_element_type=F32)
            o_run = pv if c == 0 else o_run + pv
        m_new[:, cols] = jnp.broadcast_to(m_run, (m_new.shape[0], q.shape[0]))
        o = o_run[:HEAD_DIM] * (1.0 / o_run[HEAD_DIM:HEAD_DIM + 1])
        for n, (_, _, out_slot, out_half) in enumerate(heads):
            pieces[(out_slot, out_half)] = o[:, n * tq:(n + 1) * tq]
    for j in range(o_ref.shape[2] // LANES):
        both = jnp.concatenate([pieces[(j, 0)], pieces[(j, 1)]], axis=0)
        o_ref[0, :, j * LANES:(j + 1) * LANES] = both.T.astype(BF16)


def _win_start(tile, nq, s_len):
    q0 = (tile % nq) * Q_TILE
    start = jnp.clip(q0 - WINDOW, 0, s_len - WIN_KEYS)
    place = (start - q0 + 2 * WINDOW) // WINDOW
    return pl.multiple_of(start, LANES), place


def _win_attn_steps(sink_ref, q_ref, k_ref, v_ref, tab_ref, o_ref, s_new, m_new, s_old, m_old, *,
                    nq, tiles):
    t = pl.program_id(0)
    s_len = k_ref.shape[1]
    start_new, place = _win_start(jnp.minimum(t, tiles - 1), nq, s_len)
    start_old, _ = _win_start(jnp.maximum(t - 1, 0), nq, s_len)
    kw = k_ref[0, pl.ds(start_new, WIN_KEYS), :]
    vw_t = v_ref[0, pl.ds(start_old, WIN_KEYS), :].astype(F32).T.astype(BF16)
    ones_t = jnp.ones((VT_ONES, WIN_KEYS), BF16)
    left = _lane_left((Q_TILE, LANES))
    g = B_HEADS // B_KV_HEADS
    slots = [q_ref[0, :, j * LANES:(j + 1) * LANES] for j in range(Q_SLOTS)]
    pieces = {}
    for side in range(B_KV_HEADS):
        keep = left if side == 0 else jnp.logical_not(left)
        q = jnp.concatenate([jnp.where(keep, qs, jnp.zeros_like(qs)) for qs in slots], axis=0)
        s = lax.dot_general(kw, q, (((1,), (1,)), ((), ())), preferred_element_type=F32)
        cols = slice(side * g * Q_TILE, (side + 1) * g * Q_TILE)
        sinks = [sink_ref[side * g + j] * LOG2_E for j in range(g)]
        sb, ms = [], []
        for j in range(g):
            sj = s[:, j * Q_TILE:(j + 1) * Q_TILE] + tab_ref[place, side * g + j]
            sb.append(sj)
            ms.append(jnp.maximum(jnp.max(sj, axis=0, keepdims=True), sinks[j]))
        s_new[:, cols] = jnp.concatenate(sb, axis=1)
        m_new[:, cols] = jnp.broadcast_to(jnp.concatenate(ms, axis=1), (m_new.shape[0], g * Q_TILE))

        m_prev = m_old[0:1, cols]
        p = jnp.exp2(s_old[:, cols] - m_prev)
        vt = jnp.concatenate([vw_t[side * HEAD_DIM:(side + 1) * HEAD_DIM], ones_t], axis=0)
        pv = jnp.dot(vt, p.astype(BF16), preferred_element_type=F32)
        for j in range(g):
            hc = slice(j * Q_TILE, (j + 1) * Q_TILE)
            l = pv[HEAD_DIM:HEAD_DIM + 1, hc] + jnp.exp2(sinks[j] - m_prev[:, hc])
            pieces[(j, side)] = pv[:HEAD_DIM, hc] * (1.0 / l)
    for j in range(Q_SLOTS):
        both = jnp.concatenate([pieces[(j, 0)], pieces[(j, 1)]], axis=0)
        o_ref[0, :, j * LANES:(j + 1) * LANES] = both.T.astype(BF16)


def _zero_init(t, refs):
    @pl.when(t == 0)
    def _():
        for ref in refs:
            ref[...] = jnp.zeros(ref.shape, F32)


def _dense_attn_kernel(q_ref, k_ref, v_ref, o_ref, s0, m0, s1, m1, *, plan):
    t = pl.program_id(0)
    step = functools.partial(_dense_attn_steps, q_ref, k_ref, v_ref, o_ref, plan=plan, score_keys=KEY_PIECE)
    _zero_init(t, (s1, m1))

    @pl.when(t % 2 == 0)
    def _():
        step(s0, m0, s1, m1)

    @pl.when(t % 2 == 1)
    def _():
        step(s1, m1, s0, m0)


def _attn_ab_kernel(sink_ref, qa_ref, ka_ref, va_ref, qb_ref, kb_ref, vb_ref, tab_ref, oa_ref, ob_ref,
                    sa0, ma0, sa1, ma1, sb0, mb0, sb1, mb1, *, plan, nq, tiles):
    t = pl.program_id(0)

    def step(sa_new, ma_new, sa_old, ma_old, sb_new, mb_new, sb_old, mb_old):
        _win_attn_steps(sink_ref, qb_ref, kb_ref, vb_ref, tab_ref, ob_ref, sb_new, mb_new, sb_old, mb_old,
                        nq=nq, tiles=tiles)
        _dense_attn_steps(qa_ref, ka_ref, va_ref, oa_ref, sa_new, ma_new, sa_old, ma_old, plan=plan,
                          score_keys=ka_ref.shape[1])

    _zero_init(t, (sa1, ma1, sb1, mb1))

    @pl.when(t % 2 == 0)
    def _():
        step(sa0, ma0, sa1, ma1, sb0, mb0, sb1, mb1)

    @pl.when(t % 2 == 1)
    def _():
        step(sa1, ma1, sa0, ma0, sb1, mb1, sb0, mb0)


def _pipeline_specs(b, s, q_tile):
    nq = s // q_tile
    tiles = b * nq

    def cur(t):
        return jnp.minimum(t, tiles - 1)

    def prev(t):
        return jnp.maximum(t - 1, 0)

    q_spec = lambda w: pl.BlockSpec((1, q_tile, w), lambda t: (cur(t) // nq, cur(t) % nq, 0))
    k_spec = lambda w: pl.BlockSpec((1, s, w), lambda t: (cur(t) // nq, 0, 0))
    v_spec = lambda w: pl.BlockSpec((1, s, w), lambda t: (prev(t) // nq, 0, 0))
    o_spec = lambda w: pl.BlockSpec((1, q_tile, w), lambda t: (prev(t) // nq, prev(t) % nq, 0))
    vt_spec = lambda h: pl.BlockSpec((1, h, s), lambda t: (prev(t) // nq, 0, 0))
    return nq, tiles, q_spec, k_spec, v_spec, vt_spec, o_spec


def _score_scratch_t(keys, cols):
    return [pltpu.VMEM(shape, F32) for _ in range(2) for shape in ((keys, cols), (8, cols))]


def _dense_attn(q, k, v, plan, q_tile, name):
    b, s, qw = q.shape
    ow = (max(h[2] for g in plan for h in g[0]) + 1) * LANES
    stacked_rows = sum(len(g[0]) for g in plan) * q_tile
    _, tiles, q_spec, k_spec, _, vt_spec, o_spec = _pipeline_specs(b, s, q_tile)
    return pl.pallas_call(
        functools.partial(_dense_attn_kernel, plan=plan),
        grid=(tiles + 1,),
        in_specs=[q_spec(qw), k_spec(k.shape[2]), vt_spec(v.shape[1])],
        out_specs=o_spec(ow),
        out_shape=jax.ShapeDtypeStruct((b, s, ow), BF16),
        scratch_shapes=_score_scratch_t(s, stacked_rows),
        compiler_params=_params("arbitrary"),
        name=name,
    )(q, k, v)


def _attn_ab(qa, ka, va, qb, kb, vb, tab, sink, plan):
    b, s, qw = qa.shape
    rows_a = sum(len(g[0]) for g in plan) * Q_TILE
    nq, tiles, q_spec, k_spec, v_spec, vt_spec, o_spec = _pipeline_specs(b, s, Q_TILE)
    return pl.pallas_call(
        functools.partial(_attn_ab_kernel, plan=plan, nq=nq, tiles=tiles),
        grid=(tiles + 1,),
        in_specs=[
            pl.BlockSpec(memory_space=pltpu.SMEM),
            q_spec(qw), k_spec(ka.shape[2]), vt_spec(va.shape[1]),
            q_spec(qb.shape[2]), k_spec(kb.shape[2]), v_spec(vb.shape[2]),
            _const_spec(tab.shape),
        ],
        out_specs=[o_spec(qw), o_spec(qb.shape[2])],
        out_shape=[jax.ShapeDtypeStruct((b, s, qw), BF16), jax.ShapeDtypeStruct((b, s, qb.shape[2]), BF16)],
        scratch_shapes=_score_scratch_t(s, rows_a) + _score_scratch_t(WIN_KEYS, B_HEADS * Q_TILE),
        compiler_params=_params("arbitrary"),
        name="attn_ab",
    )(sink, qa, ka, va, qb, kb, vb, tab)


def _bias_tab_kernel(rb_ref, idx_ref, o_ref):
    for p in range(WIN_PLACEMENTS):
        idx = idx_ref[p]
        for h in range(B_HEADS):
            acc = jnp.full(idx.shape, NEG_INF, F32)
            for bkt in range(NUM_BUCKETS):
                acc = jnp.where(idx == bkt, rb_ref[bkt, h] * LOG2_E, acc)
            o_ref[p, h] = acc


def _t5_bucket(rel):
    nb = NUM_BUCKETS // 2
    max_exact = nb // 2
    ret = jnp.where(rel > 0, nb, 0)
    n = jnp.abs(rel)
    large = max_exact + (jnp.log(jnp.maximum(n, 1).astype(jnp.float32) / max_exact)
                         / math.log(MAX_DISTANCE / max_exact) * (nb - max_exact)).astype(jnp.int32)
    large = jnp.minimum(large, nb - 1)
    return ret + jnp.where(n < max_exact, n, large)


def _bias_table(rel_bias):
    p = jnp.arange(WIN_PLACEMENTS)[:, None, None]
    j = jnp.arange(WIN_KEYS)[None, :, None]
    r = jnp.arange(Q_TILE)[None, None, :]
    rel = j + p * WINDOW - 2 * WINDOW - r
    idx = jnp.where(jnp.abs(rel) <= WINDOW, _t5_bucket(rel), -1).astype(jnp.int32)
    shape = (WIN_PLACEMENTS, B_HEADS, WIN_KEYS, Q_TILE)
    return pl.pallas_call(
        _bias_tab_kernel,
        in_specs=[pl.BlockSpec(memory_space=pltpu.SMEM),
                  pl.BlockSpec(idx.shape, lambda: (0, 0, 0))],
        out_specs=pl.BlockSpec(shape, lambda: (0, 0, 0, 0)),
        out_shape=jax.ShapeDtypeStruct(shape, F32),
        compiler_params=pltpu.CompilerParams(vmem_limit_bytes=V7X_VMEM_LIMIT_BYTES),
        name="win_bias_table",
    )(rel_bias, idx)


def _mix_out_kernel(x_ref, m_ref, g_ref, oa_ref, ob_ref, oc_ref, wg_ref, wa_ref, wb_ref, wc_ref,
                    wo_ref, o_ref):
    x = x_ref[0]
    d = x.shape[-1]
    h = _norm_mod(x, g_ref[...], m_ref[0, 0], m_ref[0, 1]).astype(BF16)
    merged = None
    for i, (o_br, w_br) in enumerate(((oa_ref, wa_ref), (ob_ref, wb_ref), (oc_ref, wc_ref))):
        gate = jnp.dot(h, wg_ref[:, i * d:(i + 1) * d], preferred_element_type=F32)
        term = _sigmoid(gate) * jnp.dot(o_br[0], w_br[...], preferred_element_type=F32)
        merged = term if i == 0 else merged + term
    out = jnp.dot(merged.astype(BF16), wo_ref[...], preferred_element_type=F32)
    o_ref[0] = x + m_ref[0, 2] * out


def _mix_out(x, mods, mod_idx, gain, o_a, o_b, o_c, w_gates, w_a, w_b, w_c, w_o):
    b, s, d = x.shape
    row = lambda w: pl.BlockSpec((1, ROW_TILE, w), lambda i, j: (i, j, 0))
    return pl.pallas_call(
        _mix_out_kernel,
        grid=(b, s // ROW_TILE),
        in_specs=[
            row(d),
            pl.BlockSpec((1, 3, 1, d), lambda i, j: (i, mod_idx, 0, 0)),
            _const_spec((1, d)),
            row(o_a.shape[2]), row(o_b.shape[2]), row(o_c.shape[2]),
            _const_spec(w_gates.shape), _const_spec(w_a.shape), _const_spec(w_b.shape),
            _const_spec(w_c.shape), _const_spec(w_o.shape),
        ],
        out_specs=row(d),
        out_shape=jax.ShapeDtypeStruct((b, s, d), F32),
        compiler_params=_params("arbitrary", "arbitrary"),
        name="mix_out",
    )(x, mods, gain.reshape(1, d), o_a, o_b, o_c, w_gates, w_a, w_b, w_c, w_o)


def _rope_angles(pos, dim):
    inv = ROPE_THETA ** (-jnp.arange(0, dim, 2, dtype=jnp.float32) / dim)
    ang = pos.astype(jnp.float32)[:, None] * inv[None, :]
    return jnp.cos(ang), jnp.sin(ang)


def _rope_tables(s):
    rows = s // GRID_W
    t = jnp.arange(s)
    row_pos = jnp.repeat(jnp.arange(rows), GRID_W)
    col_pos = jnp.tile(jnp.arange(GRID_W), rows)
    cr, sr = _rope_angles(row_pos, HEAD_DIM // 2)
    cc, sc = _rope_angles(col_pos, HEAD_DIM // 2)
    cos_h = jnp.concatenate([cr, cr, cc, cc], axis=-1)
    sin_h = jnp.concatenate([-sr, sr, -sc, sc], axis=-1)
    reps = LANES // HEAD_DIM
    cos_a = jnp.tile(cos_h, (1, reps))
    sin_a = jnp.tile(sin_h, (1, reps))
    cs, ss = _rope_angles(t, C_ROPE_DIM)
    ones = jnp.ones((s, C_NOPE_DIM), F32)
    zeros_lo = jnp.zeros((s, C_NOPE_DIM), F32)
    zeros_hi = jnp.zeros((s, LANES - C_QK_DIM), F32)
    t1 = jnp.concatenate([ones, cs, cs, zeros_hi], axis=-1)
    t2 = jnp.concatenate([zeros_lo, -ss, ss, zeros_hi], axis=-1)
    return cos_a, sin_a, t1, t2


def _attn_plans():
    plan_a = tuple((tuple((j, g, j, g) for j in range(Q_SLOTS)), 0, 0) for g in range(A_KV_HEADS))
    plan_c = tuple((((h, None, h // 2, h % 2),), h, h // 2) for h in range(C_HEADS))
    return plan_a, plan_c


def kernel(x, c, ada_w, ada_b, norm_ffn1, ffn1_w_gu, ffn1_w_down, norm_mix, w_in, a_q_norm, a_k_norm,
           b_sink, rel_bias, c_q_lat_norm, c_w_q_up, c_kv_lat_norm, c_w_kv_up, w_br_a, w_br_b, w_br_c,
           w_out, norm_ffn2, ffn2_w_gu, ffn2_w_down, final_norm):
    b, s, d = x.shape
    assert d == D_MODEL and s % max(Q_TILE, C_Q_TILE, ROW_TILE) == 0 and s >= WIN_KEYS and s % GRID_W == 0
    depth = ada_w.shape[0]

    mods = _ada_mods(c, ada_w, ada_b)
    tabs = _rope_tables(s)
    bias_tab = _bias_table(rel_bias)
    plan_a, plan_c = _attn_plans()

    proj_cols = _proj_columns()
    q_up_cols = _q_up_columns()
    kv_up_cols = _kv_up_columns()
    head_rows = np.asarray([h * HEAD_DIM + dd for h in _q_head_order() for dd in range(HEAD_DIM)])
    partner = np.arange(HEAD_DIM) ^ ROPE_HALF
    reps = LANES // HEAD_DIM

    w_gu1, w_down1 = ffn1_w_gu.astype(BF16), ffn1_w_down.astype(BF16)
    w_gu2, w_down2 = ffn2_w_gu.astype(BF16), ffn2_w_down.astype(BF16)
    for l in range(depth):
        x = _ffn(x, mods, 3 * l, norm_ffn1[l], w_gu1, w_down1, l)

        w_attn = _take_cols(w_in[l], proj_cols).astype(BF16)
        aqn = jnp.stack([jnp.tile(a_q_norm[l], reps), jnp.tile(a_q_norm[l][partner], reps)])
        akn = jnp.stack([jnp.tile(a_k_norm[l], reps), jnp.tile(a_k_norm[l][partner], reps)])
        wq = _take_cols(c_w_q_up[l], q_up_cols).astype(BF16)
        wkv = _take_cols(c_w_kv_up[l], kv_up_cols).astype(BF16)
        qa, ka, va, qb, kb, vb, qc, kc, vc = _proj(
            x, mods, 3 * l + 1, norm_mix[l], w_attn, aqn, akn,
            c_q_lat_norm[l].reshape(1, -1), wq, c_kv_lat_norm[l].reshape(1, -1), wkv, tabs)

        o_a, o_b = _attn_ab(qa, ka, va, qb, kb, vb, bias_tab, b_sink[l], plan_a)
        o_c = _dense_attn(qc, kc, vc, plan_c, C_Q_TILE, "attn_c")

        x = _mix_out(
            x, mods, 3 * l + 1, norm_mix[l], o_a, o_b, o_c,
            w_in[l][:, GATE_OFF:].astype(BF16),
            w_br_a[l][head_rows].astype(BF16), w_br_b[l][head_rows].astype(BF16),
            w_br_c[l].astype(BF16), w_out[l].astype(BF16))

        x = _ffn(x, mods, 3 * l + 2, norm_ffn2[l], w_gu2, w_down2, l,
                 final_gain=final_norm if l == depth - 1 else None)
    return x
```
